```python
import math
import jax, jax.numpy as jnp
from jax import lax
import numpy as np

D_MODEL = 1024
BATCH = 16
SEQ = 2048
DEPTH = 2

CHUNK = 64
Q_BLOCK = 128
N_MEM = 256
D_LRU = 512
LRU_BLOCKS = 8
LRU_BLOCK_DIM = D_LRU // LRU_BLOCKS
LRU_CONV = 4
LRU_C = 8.0
D_CONV = 512
CONV_WIDTH = 31
DA_HEADS = 4
DA_DIM = 64
DA_VDIM = 2 * DA_DIM
D_DA = DA_HEADS * DA_VDIM
REL_BUCKETS = 32
REL_MAX_DIST = 128
XA_HEADS = 4
XA_DIM = D_MODEL // XA_HEADS
D_FF = 2816
FFN_CONV = 3
N_BRANCH = 3
EPS = 1e-6

LRU_X_END = D_LRU
LRU_G_END = 2 * D_LRU
CONV_END = LRU_G_END + 2 * D_CONV
Q_END = CONV_END + DA_HEADS * 2 * DA_DIM
K_END = Q_END + DA_HEADS * 2 * DA_DIM
V_END = K_END + D_DA
D_IN = V_END

kernel_name = "hybrid_rglru_conformer_diffattn_block"


def rms_norm(x, g):
    xf = x.astype(jnp.float32)
    y = xf * lax.rsqrt(jnp.mean(xf * xf, axis=-1, keepdims=True) + EPS)
    return (y * g.astype(jnp.float32)).astype(x.dtype)


def layer_norm(x, g, b):
    xf = x.astype(jnp.float32)
    mu = jnp.mean(xf, axis=-1, keepdims=True)
    xc = xf - mu
    y = xc * lax.rsqrt(jnp.mean(xc * xc, axis=-1, keepdims=True) + EPS)
    return (y * g.astype(jnp.float32) + b.astype(jnp.float32)).astype(x.dtype)


def causal_dwconv(x, w, b):
    K, C = w.shape
    y = lax.conv_general_dilated(
        x, w[:, None, :].astype(x.dtype), window_strides=(1,), padding=[(K - 1, 0)],
        dimension_numbers=('NWC', 'WIO', 'NWC'), feature_group_count=C)
    return (y + b.astype(x.dtype)).astype(x.dtype)


def t5_bucket(rel):
    nb = REL_BUCKETS // 2
    ret = jnp.where(rel > 0, nb, 0)
    n = jnp.abs(rel)
    max_exact = nb // 2
    large = max_exact + (jnp.log(jnp.maximum(n, 1).astype(jnp.float32) / max_exact)
                         / math.log(REL_MAX_DIST / max_exact) * (nb - max_exact)).astype(jnp.int32)
    large = jnp.minimum(large, nb - 1)
    return ret + jnp.where(n < max_exact, n, large)


def rg_lru(x, wr, br, wi, bi, lam):
    B_, S_, _ = x.shape
    xf = x.astype(jnp.float32)
    xb = xf.reshape(B_, S_, LRU_BLOCKS, LRU_BLOCK_DIM)
    r = jax.nn.sigmoid(jnp.einsum('bshi,hij->bshj', xb, wr.astype(jnp.float32)).reshape(B_, S_, D_LRU) + br)
    i = jax.nn.sigmoid(jnp.einsum('bshi,hij->bshj', xb, wi.astype(jnp.float32)).reshape(B_, S_, D_LRU) + bi)
    log_a = -LRU_C * r * jax.nn.softplus(-lam.astype(jnp.float32))
    a = jnp.exp(log_a)
    u = jnp.sqrt(-jnp.expm1(2.0 * log_a)) * (i * xf)

    def combine(left, right):
        a1, b1 = left
        a2, b2 = right
        return a1 * a2, a2 * b1 + b2

    _, h = lax.associative_scan(combine, (a, u), axis=1)
    return h.astype(x.dtype)


def diff_attention(q, k, v, lam, lam_init, subln_g, rel_bias):
    B_, S_ = q.shape[0], q.shape[1]
    scale = DA_DIM ** -0.5
    qf = q.astype(jnp.float32) * scale
    kf = k.astype(jnp.float32)
    vf = v.astype(jnp.float32)
    neg = jnp.finfo(jnp.float32).min
    outs = []
    for blk in range(S_ // Q_BLOCK):
        q0, q1 = blk * Q_BLOCK, (blk + 1) * Q_BLOCK
        kb, vb = kf[:, :q1], vf[:, :q1]
        s = jnp.einsum('bqhcd,bkhcd->bchqk', qf[:, q0:q1], kb)
        qpos = jnp.arange(q0, q1, dtype=jnp.int32)
        kpos = jnp.arange(q1, dtype=jnp.int32)
        bias = jnp.transpose(rel_bias.astype(jnp.float32)[t5_bucket(kpos[None, :] - qpos[:, None])], (2, 0, 1))
        allowed = (kpos[None, :] // CHUNK) <= (qpos[:, None] // CHUNK)
        p = jax.nn.softmax(jnp.where(allowed, s + bias, neg), axis=-1)
        attn = p[:, 0] - lam * p[:, 1]
        outs.append(jnp.einsum('bhqk,bkhe->bqhe', attn, vb))
    o = jnp.concatenate(outs, axis=1)
    o = rms_norm(o, subln_g) * (1.0 - lam_init)
    return o.reshape(B_, S_, D_DA).astype(q.dtype)


def setup_inputs(seed: int = 0) -> dict:
    key = jax.random.key(seed)
    ks = iter(jax.random.split(key, 48))
    L = DEPTH

    def nrm(shape, scale):
        return jax.random.normal(next(ks), shape, jnp.float32) * scale

    def gain(shape):
        return 1.0 + nrm(shape, 0.02)

    a0 = jax.random.uniform(next(ks), (L, D_LRU), jnp.float32, 0.9, 0.999)
    lru_lambda = jnp.log(a0 / (1.0 - a0))
    return {
        "x": nrm((BATCH, SEQ, D_MODEL), 1.0),
        "mem": nrm((BATCH, N_MEM, D_MODEL), 1.0),
        "rel_bias": nrm((REL_BUCKETS, DA_HEADS), 0.5),
        "norm_mix_g": gain((L, D_MODEL)),
        "w_in": nrm((L, D_MODEL, D_IN), D_MODEL ** -0.5),
        "w_gate": nrm((L, N_BRANCH, D_MODEL, D_MODEL), D_MODEL ** -0.5),
        "b_gate": nrm((L, N_BRANCH, D_MODEL), 0.02),
        "lru_conv_w": nrm((L, LRU_CONV, D_LRU), LRU_CONV ** -0.5),
        "lru_conv_b": nrm((L, D_LRU), 0.02),
        "lru_wr": nrm((L, LRU_BLOCKS, LRU_BLOCK_DIM, LRU_BLOCK_DIM), LRU_BLOCK_DIM ** -0.5),
        "lru_br": nrm((L, D_LRU), 0.02),
        "lru_wi": nrm((L, LRU_BLOCKS, LRU_BLOCK_DIM, LRU_BLOCK_DIM), LRU_BLOCK_DIM ** -0.5),
        "lru_bi": nrm((L, D_LRU), 0.02),
        "lru_lambda": lru_lambda,
        "lru_out": nrm((L, D_LRU, D_MODEL), D_LRU ** -0.5),
        "cm_conv_w": nrm((L, CONV_WIDTH, D_CONV), CONV_WIDTH ** -0.5),
        "cm_conv_b": nrm((L, D_CONV), 0.02),
        "cm_ln_g": gain((L, D_CONV)),
        "cm_ln_b": nrm((L, D_CONV), 0.02),
        "cm_out": nrm((L, D_CONV, D_MODEL), D_CONV ** -0.5),
        "da_lambda": nrm((L, 4, DA_DIM), 0.1),
        "da_subln_g": gain((L, DA_VDIM)),
        "da_out": nrm((L, D_DA, D_MODEL), D_DA ** -0.5),
        "w_o": nrm((L, D_MODEL, D_MODEL), D_MODEL ** -0.5),
        "norm_xa_g": gain((L, D_MODEL)),
        "norm_mem_g": gain((L, D_MODEL)),
        "xa_wq": nrm((L, D_MODEL, D_MODEL), D_MODEL ** -0.5),
        "xa_wkv": nrm((L, D_MODEL, 2 * D_MODEL), D_MODEL ** -0.5),
        "xa_wo": nrm((L, D_MODEL, D_MODEL), D_MODEL ** -0.5),
        "norm_ffn_g": gain((L, D_MODEL)),
        "ffn_w1": nrm((L, D_MODEL, D_FF), D_MODEL ** -0.5),
        "ffn_w3": nrm((L, D_MODEL, D_FF), D_MODEL ** -0.5),
        "ffn_conv_w": nrm((L, FFN_CONV, D_FF), FFN_CONV ** -0.5),
        "ffn_conv_b": nrm((L, D_FF), 0.02),
        "ffn_w2": nrm((L, D_FF, D_MODEL), D_FF ** -0.5),
        "final_g": gain((D_MODEL,)),
    }


def reference(x, mem, rel_bias, norm_mix_g, w_in, w_gate, b_gate, lru_conv_w, lru_conv_b,
              lru_wr, lru_br, lru_wi, lru_bi, lru_lambda, lru_out, cm_conv_w, cm_conv_b,
              cm_ln_g, cm_ln_b, cm_out, da_lambda, da_subln_g, da_out, w_o, norm_xa_g,
              norm_mem_g, xa_wq, xa_wkv, xa_wo, norm_ffn_g, ffn_w1, ffn_w3, ffn_conv_w,
              ffn_conv_b, ffn_w2, final_g):
    B_, S_, _ = x.shape
    M_ = mem.shape[1]
    for l in range(DEPTH):
        h = rms_norm(x, norm_mix_g[l])
        z = h @ w_in[l]
        xa = causal_dwconv(z[..., :LRU_X_END], lru_conv_w[l], lru_conv_b[l])
        ya = rg_lru(xa, lru_wr[l], lru_br[l], lru_wi[l], lru_bi[l], lru_lambda[l])
        ya = ya * jax.nn.gelu(z[..., LRU_X_END:LRU_G_END])
        cz = z[..., LRU_G_END:CONV_END]
        c = cz[..., :D_CONV] * jax.nn.sigmoid(cz[..., D_CONV:])
        c = causal_dwconv(c, cm_conv_w[l], cm_conv_b[l])
        yb = jax.nn.silu(layer_norm(c, cm_ln_g[l], cm_ln_b[l]))
        q = z[..., CONV_END:Q_END].reshape(B_, S_, DA_HEADS, 2, DA_DIM)
        k = z[..., Q_END:K_END].reshape(B_, S_, DA_HEADS, 2, DA_DIM)
        v = z[..., K_END:V_END].reshape(B_, S_, DA_HEADS, DA_VDIM)
        lam_init = 0.8 - 0.6 * math.exp(-0.3 * l)
        lam_vec = da_lambda[l].astype(jnp.float32)
        lam = (jnp.exp(jnp.sum(lam_vec[0] * lam_vec[1])) - jnp.exp(jnp.sum(lam_vec[2] * lam_vec[3]))
               + lam_init)
        yc = diff_attention(q, k, v, lam, lam_init, da_subln_g[l], rel_bias)
        merged = (jax.nn.sigmoid(h @ w_gate[l, 0] + b_gate[l, 0]) * (ya @ lru_out[l])
                  + jax.nn.sigmoid(h @ w_gate[l, 1] + b_gate[l, 1]) * (yb @ cm_out[l])
                  + jax.nn.sigmoid(h @ w_gate[l, 2] + b_gate[l, 2]) * (yc @ da_out[l]))
        x = x + (merged @ w_o[l]).astype(x.dtype)

        hq = rms_norm(x, norm_xa_g[l])
        m = rms_norm(mem, norm_mem_g[l])
        qx = (hq @ xa_wq[l]).reshape(B_, S_, XA_HEADS, XA_DIM).astype(jnp.float32)
        kv = (m @ xa_wkv[l]).reshape(B_, M_, 2, XA_HEADS, XA_DIM).astype(jnp.float32)
        s = jnp.einsum('bqhd,bkhd->bhqk', qx, kv[:, :, 0]) * (XA_DIM ** -0.5)
        p = jax.nn.softmax(s, axis=-1)
        o = jnp.einsum('bhqk,bkhd->bqhd', p, kv[:, :, 1]).reshape(B_, S_, D_MODEL).astype(x.dtype)
        x = x + (o @ xa_wo[l]).astype(x.dtype)

        hf = rms_norm(x, norm_ffn_g[l])
        a = causal_dwconv(hf @ ffn_w1[l], ffn_conv_w[l], ffn_conv_b[l])
        x = x + ((jax.nn.silu(a) * (hf @ ffn_w3[l])) @ ffn_w2[l]).astype(x.dtype)
    return rms_norm(x, final_g)
```

```python
import functools
import math

import jax
import jax.numpy as jnp
import numpy as np
from jax import lax
from jax.experimental import pallas as pl
from jax.experimental.pallas import tpu as pltpu

F32 = jnp.float32
BF16 = jnp.bfloat16

D_MODEL = 1024
CHUNK = 64
D_LRU = 512
LRU_BLOCKS = 8
LRU_CONV = 4
LRU_C = 8.0
D_CONV = 512
CONV_WIDTH = 31
DA_HEADS = 4
DA_DIM = 64
DA_VDIM = 2 * DA_DIM
D_DA = DA_HEADS * DA_VDIM
REL_BUCKETS = 32
REL_MAX_DIST = 128
XA_HEADS = 4
XA_DIM = D_MODEL // XA_HEADS
D_FF = 2816
FFN_CONV = 3
N_BRANCH = 3
EPS = 1e-6

N_GATE = N_BRANCH * D_MODEL
COL_LRU_X = N_GATE
COL_LRU_G = COL_LRU_X + D_LRU
COL_CONV_A = COL_LRU_G + D_LRU
COL_CONV_G = COL_CONV_A + D_CONV
COL_Q = COL_CONV_G + D_CONV
COL_K = COL_Q + DA_HEADS * 2 * DA_DIM
COL_V = COL_K + DA_HEADS * 2 * DA_DIM
N_MIX = COL_V + D_DA

SUBLANES = 8
LANES = 128
VMEM_LIMIT = 56 * 1024 * 1024
ROW_TILE = 512
MIX_COL_CHUNK = 512
SEQ_TILE = 512
CONV_ROWS = 32
CONV_HALO = 32
DA_TQ = 256
DA_TK = 256
FFN_COL_CHUNK = 256
MASK_VALUE = -1e30


def _resident(shape):
    nd = len(shape)
    return pl.BlockSpec(shape, lambda *_: (0,) * nd, pipeline_mode=pl.Buffered(1))


def _params(semantics):
    return pltpu.CompilerParams(dimension_semantics=semantics, vmem_limit_bytes=VMEM_LIMIT)


def _rms_norm(x, g):
    return x * lax.rsqrt(jnp.mean(x * x, axis=-1, keepdims=True) + EPS) * g


def _sigmoid(x):
    return 1.0 / (1.0 + jnp.exp(-x))


def _gelu_tanh(x):
    c = math.sqrt(2.0 / math.pi)
    return 0.5 * x * (1.0 + jnp.tanh(c * (x + 0.044715 * (x * x * x))))


def _norm_proj_kernel(x_ref, g_ref, w_ref, b_ref, o_ref, *, n_sig, chunk):
    h = _rms_norm(x_ref[...], g_ref[...]).astype(BF16)
    n_out = o_ref.shape[1]
    for c0 in range(0, n_out, chunk):
        y = jnp.dot(h, w_ref[:, c0:c0 + chunk], preferred_element_type=F32)
        if c0 < n_sig:
            y = _sigmoid(y + b_ref[:, c0:c0 + chunk])
        o_ref[:, c0:c0 + chunk] = y.astype(o_ref.dtype)


def _norm_proj(x2d, g, w, b, n_sig, name):
    n, d = x2d.shape
    n_out = w.shape[1]
    assert n % ROW_TILE == 0 and n_out % MIX_COL_CHUNK == 0 and n_sig % MIX_COL_CHUNK == 0
    return pl.pallas_call(
        functools.partial(_norm_proj_kernel, n_sig=n_sig, chunk=MIX_COL_CHUNK),
        grid=(n // ROW_TILE,),
        in_specs=[
            pl.BlockSpec((ROW_TILE, d), lambda i: (i, 0)),
            _resident((1, d)),
            _resident((d, n_out)),
            _resident((1, b.shape[1])),
        ],
        out_specs=pl.BlockSpec((ROW_TILE, n_out), lambda i: (i, 0)),
        out_shape=jax.ShapeDtypeStruct((n, n_out), BF16),
        compiler_params=_params(("parallel",)),
        name=name,
    )(x2d, g, w, b)


def _lru_kernel(zx_ref, zg_ref, cw_ref, cb_ref, wri_ref, bri_ref, lam_ref, o_ref,
                xe_ref, a_ref, u_ref, carry_ref):
    ts = zx_ref.shape[0]
    halo = SUBLANES

    @pl.when(pl.program_id(1) == 0)
    def _():
        xe_ref[0:halo, :] = jnp.zeros((halo, D_LRU), F32)
        carry_ref[...] = jnp.zeros((SUBLANES, D_LRU), F32)

    xe_ref[halo:halo + ts, :] = zx_ref[...].astype(F32)
    xa = cb_ref[...] + cw_ref[LRU_CONV - 1:LRU_CONV, :] * xe_ref[halo:halo + ts, :]
    for j in range(LRU_CONV - 1):
        off = halo - (LRU_CONV - 1) + j
        xa = xa + cw_ref[j:j + 1, :] * xe_ref[off:off + ts, :]
    xe_ref[0:halo, :] = xe_ref[ts:ts + halo, :]

    ri = _sigmoid(jnp.dot(xa.astype(BF16), wri_ref[...], preferred_element_type=F32) + bri_ref[...])
    r = ri[:, :D_LRU]
    gi = ri[:, D_LRU:]
    nlam = -lam_ref[...]
    softplus = jnp.maximum(nlam, 0.0) + jnp.log(1.0 + jnp.exp(-jnp.abs(nlam)))
    log_a = (-LRU_C) * r * softplus
    a = jnp.exp(log_a)
    u = jnp.sqrt(1.0 - jnp.exp(2.0 * log_a)) * (gi * xa)

    row = lax.broadcasted_iota(jnp.int32, (ts, D_LRU), 0) & (SUBLANES - 1)
    s = 1
    while s < SUBLANES:
        keep = row >= s
        u = jnp.where(keep, a * pltpu.roll(u, s, 0) + u, u)
        a = jnp.where(keep, a * pltpu.roll(a, s, 0), a)
        s *= 2
    a_ref[...] = a
    u_ref[...] = u

    carry = carry_ref[...]
    for g0 in range(0, ts, SUBLANES):
        h = a_ref[g0:g0 + SUBLANES, :] * carry + u_ref[g0:g0 + SUBLANES, :]
        u_ref[g0:g0 + SUBLANES, :] = h
        carry = jnp.broadcast_to(h[SUBLANES - 1:SUBLANES, :], (SUBLANES, D_LRU))
    carry_ref[...] = carry

    o_ref[...] = (u_ref[...] * _gelu_tanh(zg_ref[...].astype(F32))).astype(o_ref.dtype)


def _lru_branch(zmix, batch, seq, cw, cb, wri, bri, lam):
    nt = seq // SEQ_TILE
    return pl.pallas_call(
        _lru_kernel,
        grid=(batch, nt),
        in_specs=[
            pl.BlockSpec((SEQ_TILE, D_LRU), lambda b, t: (b * nt + t, COL_LRU_X // D_LRU)),
            pl.BlockSpec((SEQ_TILE, D_LRU), lambda b, t: (b * nt + t, COL_LRU_G // D_LRU)),
            _resident((LRU_CONV, D_LRU)),
            _resident((1, D_LRU)),
            _resident((D_LRU, 2 * D_LRU)),
            _resident((1, 2 * D_LRU)),
            _resident((1, D_LRU)),
        ],
        out_specs=pl.BlockSpec((SEQ_TILE, D_LRU), lambda b, t: (b * nt + t, 0)),
        out_shape=jax.ShapeDtypeStruct((batch * seq, D_LRU), BF16),
        scratch_shapes=[
            pltpu.VMEM((SEQ_TILE + SUBLANES, D_LRU), F32),
            pltpu.VMEM((SEQ_TILE, D_LRU), F32),
            pltpu.VMEM((SEQ_TILE, D_LRU), F32),
            pltpu.VMEM((SUBLANES, D_LRU), F32),
        ],
        compiler_params=_params(("parallel", "arbitrary")),
        name="lru_branch",
    )(zmix, zmix, cw, cb, wri, bri, lam)


def _cconv_kernel(ca_ref, cg_ref, w_ref, b_ref, lg_ref, lb_ref, o_ref, ce_ref):
    ts = ca_ref.shape[0]

    @pl.when(pl.program_id(1) == 0)
    def _():
        ce_ref[0:CONV_HALO, :] = jnp.zeros((CONV_HALO, D_CONV), F32)

    ce_ref[CONV_HALO:CONV_HALO + ts, :] = ca_ref[...].astype(F32) * _sigmoid(cg_ref[...].astype(F32))
    for r0 in range(0, ts, CONV_ROWS):
        acc = jnp.broadcast_to(b_ref[...], (CONV_ROWS, D_CONV))
        for j in range(CONV_WIDTH):
            off = CONV_HALO + r0 - (CONV_WIDTH - 1) + j
            acc = acc + w_ref[j:j + 1, :] * ce_ref[off:off + CONV_ROWS, :]
        mu = jnp.mean(acc, axis=-1, keepdims=True)
        xc = acc - mu
        y = xc * lax.rsqrt(jnp.mean(xc * xc, axis=-1, keepdims=True) + EPS) * lg_ref[...] + lb_ref[...]
        o_ref[r0:r0 + CONV_ROWS, :] = (y * _sigmoid(y)).astype(o_ref.dtype)
    ce_ref[0:CONV_HALO, :] = ce_ref[ts:ts + CONV_HALO, :]


def _cconv_branch(zmix, batch, seq, w, b, lg, lb):
    nt = seq // SEQ_TILE
    return pl.pallas_call(
        _cconv_kernel,
        grid=(batch, nt),
        in_specs=[
            pl.BlockSpec((SEQ_TILE, D_CONV), lambda b_, t: (b_ * nt + t, COL_CONV_A // D_CONV)),
            pl.BlockSpec((SEQ_TILE, D_CONV), lambda b_, t: (b_ * nt + t, COL_CONV_G // D_CONV)),
            _resident((CONV_WIDTH, D_CONV)),
            _resident((1, D_CONV)),
            _resident((1, D_CONV)),
            _resident((1, D_CONV)),
        ],
        out_specs=pl.BlockSpec((SEQ_TILE, D_CONV), lambda b_, t: (b_ * nt + t, 0)),
        out_shape=jax.ShapeDtypeStruct((batch * seq, D_CONV), BF16),
        scratch_shapes=[pltpu.VMEM((SEQ_TILE + CONV_HALO, D_CONV), F32)],
        compiler_params=_params(("parallel", "arbitrary")),
        name="cconv_branch",
    )(zmix, zmix, w, b, lg, lb)


def _t5_bucket(rel):
    nb = REL_BUCKETS // 2
    ret = jnp.where(rel > 0, nb, 0)
    n = jnp.abs(rel)
    max_exact = nb // 2
    large = max_exact + (jnp.log(jnp.maximum(n, 1).astype(F32) / max_exact)
                         / math.log(REL_MAX_DIST / max_exact) * (nb - max_exact)).astype(jnp.int32)
    large = jnp.minimum(large, nb - 1)
    return ret + jnp.where(n < max_exact, n, large)


def _bias_tiles_kernel(tbl_ref, bucket_ref, o_ref, *, far_bucket):
    h = pl.program_id(0)
    bucket = bucket_ref[...]
    acc = jnp.zeros(bucket.shape, F32)
    for bkt in range(REL_BUCKETS):
        acc = jnp.where(bucket == bkt, tbl_ref[bkt, h], acc)
    acc = acc - tbl_ref[far_bucket, h]
    o_ref[...] = jnp.where(bucket < 0, MASK_VALUE, acc)


def _bias_tiles(rel_bias):
    qpos = jnp.arange(DA_TQ, dtype=jnp.int32)[:, None]
    kpos = jnp.arange(DA_TK, dtype=jnp.int32)[None, :]
    prev = _t5_bucket(kpos - DA_TK - qpos)
    diag = jnp.where(kpos // CHUNK <= qpos // CHUNK, _t5_bucket(kpos - qpos), -1)
    buckets = jnp.stack([prev, diag])
    assert DA_TK >= REL_MAX_DIST and DA_TQ == DA_TK
    far_bucket = REL_BUCKETS // 2 - 1
    return pl.pallas_call(
        functools.partial(_bias_tiles_kernel, far_bucket=far_bucket),
        grid=(DA_HEADS,),
        in_specs=[
            pl.BlockSpec(memory_space=pltpu.SMEM),
            pl.BlockSpec((2, DA_TQ, DA_TK), lambda h: (0, 0, 0)),
        ],
        out_specs=pl.BlockSpec((None, 2, DA_TQ, DA_TK), lambda h: (h, 0, 0, 0)),
        out_shape=jax.ShapeDtypeStruct((DA_HEADS, 2, DA_TQ, DA_TK), F32),
        compiler_params=_params(("parallel",)),
        name="bias_tiles",
    )(rel_bias, buckets)


def _da_kernel(q_ref, k_ref, v_ref, bias_ref, lam_ref, sg_ref, o_ref, m_ref, l_ref, acc_ref, *, lam_init):
    i = pl.program_id(2)
    tq, tk = DA_TQ, DA_TK
    q = q_ref[...] * jnp.asarray(DA_DIM ** -0.5, q_ref.dtype)
    lane = lax.broadcasted_iota(jnp.int32, q.shape, 1)
    zero = jnp.zeros_like(q)
    q2 = jnp.concatenate([jnp.where(lane < DA_DIM, q, zero), jnp.where(lane >= DA_DIM, q, zero)], axis=0)

    m_ref[...] = jnp.full(m_ref.shape, MASK_VALUE, F32)
    l_ref[...] = jnp.zeros(l_ref.shape, F32)
    acc_ref[...] = jnp.zeros(acc_ref.shape, F32)

    def step(j, bias):
        k0 = pl.multiple_of(j * tk, tk)
        kb = k_ref[pl.ds(k0, tk), :]
        vb = v_ref[pl.ds(k0, tk), :]
        s = lax.dot_general(q2, kb, (((1,), (1,)), ((), ())), preferred_element_type=F32)
        if bias is not None:
            s = s + jnp.concatenate([bias, bias], axis=0)
        m_prev = m_ref[...]
        m_new = jnp.maximum(m_prev, jnp.max(s, axis=-1, keepdims=True))
        alpha = jnp.exp(m_prev - m_new)
        p = jnp.exp(s - m_new)
        l_ref[...] = alpha * l_ref[...] + jnp.sum(p, axis=-1, keepdims=True)
        acc_ref[...] = alpha * acc_ref[...] + jnp.dot(p.astype(BF16), vb, preferred_element_type=F32)
        m_ref[...] = m_new

    def far_body(j, c):
        step(j, None)
        return c

    lax.fori_loop(0, jnp.maximum(i - 1, 0), far_body, 0)

    @pl.when(i >= 1)
    def _():
        step(i - 1, bias_ref[0])

    step(i, bias_ref[1])

    o = acc_ref[...] / l_ref[...]
    lv = lam_ref[...]
    lam = (jnp.exp(jnp.sum(lv[0:1, :] * lv[1:2, :], axis=-1, keepdims=True))
           - jnp.exp(jnp.sum(lv[2:3, :] * lv[3:4, :], axis=-1, keepdims=True)) + lam_init)
    d = o[:tq, :] - lam * o[tq:, :]
    o_ref[...] = (_rms_norm(d, sg_ref[...]) * (1.0 - lam_init)).astype(o_ref.dtype)


def _diff_attention(zmix, batch, seq, bias_tiles, lam_vec, subln_g, lam_init):
    nq = seq // DA_TQ
    blk = 2 * DA_DIM
    return pl.pallas_call(
        functools.partial(_da_kernel, lam_init=lam_init),
        grid=(batch, DA_HEADS, nq),
        in_specs=[
            pl.BlockSpec((DA_TQ, blk), lambda b, h, i: (b * nq + i, COL_Q // blk + h)),
            pl.BlockSpec((seq, blk), lambda b, h, i: (b, COL_K // blk + h)),
            pl.BlockSpec((seq, DA_VDIM), lambda b, h, i: (b, COL_V // DA_VDIM + h)),
            pl.BlockSpec((None, 2, DA_TQ, DA_TK), lambda b, h, i: (h, 0, 0, 0)),
            _resident((4, DA_DIM)),
            _resident((1, DA_VDIM)),
        ],
        out_specs=pl.BlockSpec((DA_TQ, DA_VDIM), lambda b, h, i: (b * nq + i, h)),
        out_shape=jax.ShapeDtypeStruct((batch * seq, D_DA), BF16),
        scratch_shapes=[
            pltpu.VMEM((2 * DA_TQ, 1), F32),
            pltpu.VMEM((2 * DA_TQ, 1), F32),
            pltpu.VMEM((2 * DA_TQ, DA_VDIM), F32),
        ],
        compiler_params=_params(("parallel", "parallel", "arbitrary")),
        name="diff_attention",
    )(zmix, zmix, zmix, bias_tiles, lam_vec, subln_g)


def _merge_kernel(x_ref, ya_ref, yb_ref, yc_ref, g0_ref, g1_ref, g2_ref,
                  wa_ref, wb_ref, wc_ref, wo_ref, o_ref):
    merged = (g0_ref[...].astype(F32) * jnp.dot(ya_ref[...], wa_ref[...], preferred_element_type=F32)
              + g1_ref[...].astype(F32) * jnp.dot(yb_ref[...], wb_ref[...], preferred_element_type=F32)
              + g2_ref[...].astype(F32) * jnp.dot(yc_ref[...], wc_ref[...], preferred_element_type=F32))
    o_ref[...] = x_ref[...] + jnp.dot(merged.astype(BF16), wo_ref[...], preferred_element_type=F32)


def _merge(x2d, ya, yb, yc, zmix, wa, wb, wc, wo):
    n = x2d.shape[0]
    row = lambda i: (i, 0)
    return pl.pallas_call(
        _merge_kernel,
        grid=(n // ROW_TILE,),
        in_specs=[
            pl.BlockSpec((ROW_TILE, D_MODEL), row),
            pl.BlockSpec((ROW_TILE, D_LRU), row),
            pl.BlockSpec((ROW_TILE, D_CONV), row),
            pl.BlockSpec((ROW_TILE, D_DA), row),
            pl.BlockSpec((ROW_TILE, D_MODEL), lambda i: (i, 0)),
            pl.BlockSpec((ROW_TILE, D_MODEL), lambda i: (i, 1)),
            pl.BlockSpec((ROW_TILE, D_MODEL), lambda i: (i, 2)),
            _resident((D_LRU, D_MODEL)),
            _resident((D_CONV, D_MODEL)),
            _resident((D_DA, D_MODEL)),
            _resident((D_MODEL, D_MODEL)),
        ],
        out_specs=pl.BlockSpec((ROW_TILE, D_MODEL), row),
        out_shape=jax.ShapeDtypeStruct((n, D_MODEL), F32),
        compiler_params=_params(("parallel",)),
        name="merge",
    )(x2d, ya, yb, yc, zmix, zmix, zmix, wa, wb, wc, wo)


def _xattn_kernel(x_ref, g_ref, wq_ref, kv_ref, wo_ref, o_ref, oc_ref):
    x = x_ref[...]
    hq = _rms_norm(x, g_ref[...]).astype(BF16)
    q = (jnp.dot(hq, wq_ref[...], preferred_element_type=F32) * (XA_DIM ** -0.5)).astype(BF16)
    for h in range(XA_HEADS):
        c0 = h * XA_DIM
        kh = kv_ref[:, c0:c0 + XA_DIM]
        vh = kv_ref[:, D_MODEL + c0:D_MODEL + c0 + XA_DIM]
        s = lax.dot_general(q[:, c0:c0 + XA_DIM], kh, (((1,), (1,)), ((), ())), preferred_element_type=F32)
        p = jnp.exp(s - jnp.max(s, axis=-1, keepdims=True))
        l = jnp.sum(p, axis=-1, keepdims=True)
        o = jnp.dot(p.astype(BF16), vh, preferred_element_type=F32) / l
        oc_ref[:, c0:c0 + XA_DIM] = o.astype(BF16)
    o_ref[...] = x + jnp.dot(oc_ref[...], wo_ref[...], preferred_element_type=F32)


def _cross_attention(x2d, batch, seq, g, wq, kv, wo):
    nt = seq // ROW_TILE
    n_mem = kv.shape[0] // batch
    return pl.pallas_call(
        _xattn_kernel,
        grid=(batch, nt),
        in_specs=[
            pl.BlockSpec((ROW_TILE, D_MODEL), lambda b, t: (b * nt + t, 0)),
            _resident((1, D_MODEL)),
            _resident((D_MODEL, D_MODEL)),
            pl.BlockSpec((n_mem, 2 * D_MODEL), lambda b, t: (b, 0)),
            _resident((D_MODEL, D_MODEL)),
        ],
        out_specs=pl.BlockSpec((ROW_TILE, D_MODEL), lambda b, t: (b * nt + t, 0)),
        out_shape=jax.ShapeDtypeStruct(x2d.shape, F32),
        scratch_shapes=[pltpu.VMEM((ROW_TILE, D_MODEL), BF16)],
        compiler_params=_params(("parallel", "arbitrary")),
        name="cross_attention",
    )(x2d, g, wq, kv, wo)


def _ffn_kernel(x_ref, g_ref, w1_ref, w3_ref, cw_ref, cb_ref, w2_ref, fg_ref, o_ref,
                ae_ref, halo_ref, hm_ref, *, final_norm):
    ts = x_ref.shape[0]
    halo = SUBLANES

    @pl.when(pl.program_id(1) == 0)
    def _():
        halo_ref[...] = jnp.zeros(halo_ref.shape, F32)

    x = x_ref[...]
    hf = _rms_norm(x, g_ref[...]).astype(BF16)
    for c0 in range(0, D_FF, FFN_COL_CHUNK):
        cs = slice(c0, c0 + FFN_COL_CHUNK)
        a = jnp.dot(hf, w1_ref[:, cs], preferred_element_type=F32)
        ae_ref[0:halo, :] = halo_ref[:, cs]
        ae_ref[halo:halo + ts, :] = a
        halo_ref[:, cs] = a[ts - halo:ts, :]
        y = cb_ref[:, cs] + cw_ref[FFN_CONV - 1:FFN_CONV, cs] * a
        for j in range(FFN_CONV - 1):
            off = halo - (FFN_CONV - 1) + j
            y = y + cw_ref[j:j + 1, cs] * ae_ref[off:off + ts, :]
        up = jnp.dot(hf, w3_ref[:, cs], preferred_element_type=F32)
        hm_ref[:, cs] = (y * _sigmoid(y) * up).astype(BF16)
    out = x + jnp.dot(hm_ref[...], w2_ref[...], preferred_element_type=F32)
    if final_norm:
        out = _rms_norm(out, fg_ref[...])
    o_ref[...] = out


def _ffn(x2d, batch, seq, g, w1, w3, cw, cb, w2, fg, final_norm):
    nt = seq // SEQ_TILE
    return pl.pallas_call(
        functools.partial(_ffn_kernel, final_norm=final_norm),
        grid=(batch, nt),
        in_specs=[
            pl.BlockSpec((SEQ_TILE, D_MODEL), lambda b, t: (b * nt + t, 0)),
            _resident((1, D_MODEL)),
            _resident((D_MODEL, D_FF)),
            _resident((D_MODEL, D_FF)),
            _resident((FFN_CONV, D_FF)),
            _resident((1, D_FF)),
            _resident((D_FF, D_MODEL)),
            _resident((1, D_MODEL)),
        ],
        out_specs=pl.BlockSpec((SEQ_TILE, D_MODEL), lambda b, t: (b * nt + t, 0)),
        out_shape=jax.ShapeDtypeStruct(x2d.shape, F32),
        scratch_shapes=[
            pltpu.VMEM((SEQ_TILE + SUBLANES, FFN_COL_CHUNK), F32),
            pltpu.VMEM((SUBLANES, D_FF), F32),
            pltpu.VMEM((SEQ_TILE, D_FF), BF16),
        ],
        compiler_params=_params(("parallel", "arbitrary")),
        name="ffn_final" if final_norm else "ffn",
    )(x2d, g, w1, w3, cw, cb, w2, fg)


def _block_diag(w):
    nb, bi, bj = w.shape
    eye = jnp.eye(nb, dtype=w.dtype)
    return jnp.einsum('hij,hg->higj', w, eye).reshape(nb * bi, nb * bj)


def kernel(x, mem, rel_bias, norm_mix_g, w_in, w_gate, b_gate, lru_conv_w, lru_conv_b, lru_wr, lru_br, lru_wi, lru_bi, lru_lambda, lru_out, cm_conv_w, cm_conv_b, cm_ln_g, cm_ln_b, cm_out, da_lambda, da_subln_g, da_out, w_o, norm_xa_g, norm_mem_g, xa_wq, xa_wkv, xa_wo, norm_ffn_g, ffn_w1, ffn_w3, ffn_conv_w, ffn_conv_b, ffn_w2, final_g):
    batch, seq, d = x.shape
    n_mem = mem.shape[1]
    depth = w_in.shape[0]
    assert d == D_MODEL and seq % SEQ_TILE == 0 and seq % ROW_TILE == 0 and seq % DA_TQ == 0

    bias_tiles = _bias_tiles(rel_bias)
    x2d = x.reshape(batch * seq, d)
    mem2d = mem.reshape(batch * n_mem, d)
    row = lambda v: v.reshape(1, -1)
    no_bias = jnp.zeros((1, MIX_COL_CHUNK), F32)

    for l in range(depth):
        w_mix = jnp.concatenate([w_gate[l, 0], w_gate[l, 1], w_gate[l, 2], w_in[l]], axis=1).astype(BF16)
        zmix = _norm_proj(x2d, row(norm_mix_g[l]), w_mix, row(b_gate[l]), N_GATE, "mix_proj")
        wri = jnp.concatenate([_block_diag(lru_wr[l]), _block_diag(lru_wi[l])], axis=1).astype(BF16)
        bri = jnp.concatenate([lru_br[l], lru_bi[l]]).reshape(1, -1)
        ya = _lru_branch(zmix, batch, seq, lru_conv_w[l], row(lru_conv_b[l]), wri, bri, row(lru_lambda[l]))
        yb = _cconv_branch(zmix, batch, seq, cm_conv_w[l], row(cm_conv_b[l]), row(cm_ln_g[l]), row(cm_ln_b[l]))
        lam_init = 0.8 - 0.6 * math.exp(-0.3 * l)
        yc = _diff_attention(zmix, batch, seq, bias_tiles, da_lambda[l], row(da_subln_g[l]), lam_init)
        x2d = _merge(x2d, ya, yb, yc, zmix, lru_out[l].astype(BF16), cm_out[l].astype(BF16),
                     da_out[l].astype(BF16), w_o[l].astype(BF16))
        kv = _norm_proj(mem2d, row(norm_mem_g[l]), xa_wkv[l].astype(BF16), no_bias, 0, "mem_kv")
        x2d = _cross_attention(x2d, batch, seq, row(norm_xa_g[l]), xa_wq[l].astype(BF16), kv,
                               xa_wo[l].astype(BF16))
        x2d = _ffn(x2d, batch, seq, row(norm_ffn_g[l]), ffn_w1[l].astype(BF16), ffn_w3[l].astype(BF16),
                   ffn_conv_w[l], row(ffn_conv_b[l]), ffn_w2[l].astype(BF16), row(final_g),
                   final_norm=(l == depth - 1))
    return x2d.reshape(batch, seq, d)
```

```python
import functools
import math

import jax
import jax.numpy as jnp
import numpy as np
from jax import lax
from jax.experimental import pallas as pl
from jax.experimental.pallas import tpu as pltpu

F32 = jnp.float32
BF16 = jnp.bfloat16

D_MODEL = 1024
CHUNK = 64
D_LRU = 512
LRU_BLOCKS = 8
LRU_CONV = 4
LRU_C = 8.0
D_CONV = 512
CONV_WIDTH = 31
DA_HEADS = 4
DA_DIM = 64
DA_VDIM = 2 * DA_DIM
D_DA = DA_HEADS * DA_VDIM
REL_BUCKETS = 32
REL_MAX_DIST = 128
XA_HEADS = 4
XA_DIM = D_MODEL // XA_HEADS
D_FF = 2816
FFN_CONV = 3
N_BRANCH = 3
EPS = 1e-6

N_GATE = N_BRANCH * D_MODEL
COL_LRU_X = N_GATE
COL_LRU_G = COL_LRU_X + D_LRU
COL_CONV_A = COL_LRU_G + D_LRU
COL_CONV_G = COL_CONV_A + D_CONV
COL_Q = COL_CONV_G + D_CONV
COL_K = COL_Q + DA_HEADS * 2 * DA_DIM
COL_V = COL_K + DA_HEADS * 2 * DA_DIM
N_MIX = COL_V + D_DA

SUBLANES = 8
LANES = 128
VMEM_LIMIT = 56 * 1024 * 1024
ROW_TILE = 512
MIX_COL_CHUNK = 512
SEQ_TILE = 512
CONV_ROWS = 32
CONV_HALO = 32
DA_TQ = 256
DA_TK = 256
FFN_COL_CHUNK = 256
MASK_VALUE = -1e30


def _resident(shape):
    nd = len(shape)
    return pl.BlockSpec(shape, lambda *_: (0,) * nd, pipeline_mode=pl.Buffered(1))


def _params(semantics):
    return pltpu.CompilerParams(dimension_semantics=semantics, vmem_limit_bytes=VMEM_LIMIT)


def _rms_norm(x, g):
    return x * lax.rsqrt(jnp.mean(x * x, axis=-1, keepdims=True) + EPS) * g


def _sigmoid(x):
    return 1.0 / (1.0 + jnp.exp(-x))


def _gelu_tanh(x):
    c = math.sqrt(2.0 / math.pi)
    return 0.5 * x * (1.0 + jnp.tanh(c * (x + 0.044715 * (x * x * x))))


def _norm_proj_kernel(x_ref, g_ref, w_ref, b_ref, o_ref, *, n_sig, chunk):
    h = _rms_norm(x_ref[...], g_ref[...]).astype(BF16)
    n_out = o_ref.shape[1]
    for c0 in range(0, n_out, chunk):
        y = jnp.dot(h, w_ref[:, c0:c0 + chunk], preferred_element_type=F32)
        if c0 < n_sig:
            y = _sigmoid(y + b_ref[:, c0:c0 + chunk])
        o_ref[:, c0:c0 + chunk] = y.astype(o_ref.dtype)


def _norm_proj(x2d, g, w, b, n_sig, name):
    n, d = x2d.shape
    n_out = w.shape[1]
    assert n % ROW_TILE == 0 and n_out % MIX_COL_CHUNK == 0 and n_sig % MIX_COL_CHUNK == 0
    return pl.pallas_call(
        functools.partial(_norm_proj_kernel, n_sig=n_sig, chunk=MIX_COL_CHUNK),
        grid=(n // ROW_TILE,),
        in_specs=[
            pl.BlockSpec((ROW_TILE, d), lambda i: (i, 0)),
            _resident((1, d)),
            _resident((d, n_out)),
            _resident((1, b.shape[1])),
        ],
        out_specs=pl.BlockSpec((ROW_TILE, n_out), lambda i: (i, 0)),
        out_shape=jax.ShapeDtypeStruct((n, n_out), BF16),
        compiler_params=_params(("parallel",)),
        name=name,
    )(x2d, g, w, b)


def _lru_kernel(zx_ref, zg_ref, cw_ref, cb_ref, wri_ref, bri_ref, lam_ref, o_ref,
                xe_ref, a_ref, u_ref, carry_ref):
    ts = zx_ref.shape[0]
    halo = SUBLANES

    @pl.when(pl.program_id(1) == 0)
    def _():
        xe_ref[0:halo, :] = jnp.zeros((halo, D_LRU), F32)
        carry_ref[...] = jnp.zeros((SUBLANES, D_LRU), F32)

    xe_ref[halo:halo + ts, :] = zx_ref[...].astype(F32)
    xa = cb_ref[...] + cw_ref[LRU_CONV - 1:LRU_CONV, :] * xe_ref[halo:halo + ts, :]
    for j in range(LRU_CONV - 1):
        off = halo - (LRU_CONV - 1) + j
        xa = xa + cw_ref[j:j + 1, :] * xe_ref[off:off + ts, :]
    xe_ref[0:halo, :] = xe_ref[ts:ts + halo, :]

    ri = _sigmoid(jnp.dot(xa.astype(BF16), wri_ref[...], preferred_element_type=F32) + bri_ref[...])
    r = ri[:, :D_LRU]
    gi = ri[:, D_LRU:]
    nlam = -lam_ref[...]
    softplus = jnp.maximum(nlam, 0.0) + jnp.log(1.0 + jnp.exp(-jnp.abs(nlam)))
    log_a = (-LRU_C) * r * softplus
    a = jnp.exp(log_a)
    u = jnp.sqrt(1.0 - jnp.exp(2.0 * log_a)) * (gi * xa)

    row = lax.broadcasted_iota(jnp.int32, (ts, D_LRU), 0) & (SUBLANES - 1)
    s = 1
    while s < SUBLANES:
        keep = row >= s
        u = jnp.where(keep, a * pltpu.roll(u, s, 0) + u, u)
        a = jnp.where(keep, a * pltpu.roll(a, s, 0), a)
        s *= 2
    a_ref[...] = a
    u_ref[...] = u

    carry = carry_ref[...]
    for g0 in range(0, ts, SUBLANES):
        h = a_ref[g0:g0 + SUBLANES, :] * carry + u_ref[g0:g0 + SUBLANES, :]
        u_ref[g0:g0 + SUBLANES, :] = h
        carry = jnp.broadcast_to(h[SUBLANES - 1:SUBLANES, :], (SUBLANES, D_LRU))
    carry_ref[...] = carry

    o_ref[...] = (u_ref[...] * _gelu_tanh(zg_ref[...].astype(F32))).astype(o_ref.dtype)


def _lru_branch(zmix, batch, seq, cw, cb, wri, bri, lam):
    nt = seq // SEQ_TILE
    return pl.pallas_call(
        _lru_kernel,
        grid=(batch, nt),
        in_specs=[
            pl.BlockSpec((SEQ_TILE, D_LRU), lambda b, t: (b * nt + t, COL_LRU_X // D_LRU)),
            pl.BlockSpec((SEQ_TILE, D_LRU), lambda b, t: (b * nt + t, COL_LRU_G // D_LRU)),
            _resident((LRU_CONV, D_LRU)),
            _resident((1, D_LRU)),
            _resident((D_LRU, 2 * D_LRU)),
            _resident((1, 2 * D_LRU)),
            _resident((1, D_LRU)),
        ],
        out_specs=pl.BlockSpec((SEQ_TILE, D_LRU), lambda b, t: (b * nt + t, 0)),
        out_shape=jax.ShapeDtypeStruct((batch * seq, D_LRU), BF16),
        scratch_shapes=[
            pltpu.VMEM((SEQ_TILE + SUBLANES, D_LRU), F32),
            pltpu.VMEM((SEQ_TILE, D_LRU), F32),
            pltpu.VMEM((SEQ_TILE, D_LRU), F32),
            pltpu.VMEM((SUBLANES, D_LRU), F32),
        ],
        compiler_params=_params(("parallel", "arbitrary")),
        name="lru_branch",
    )(zmix, zmix, cw, cb, wri, bri, lam)


def _cconv_kernel(ca_ref, cg_ref, w_ref, b_ref, lg_ref, lb_ref, o_ref, ce_ref):
    ts = ca_ref.shape[0]

    @pl.when(pl.program_id(1) == 0)
    def _():
        ce_ref[0:CONV_HALO, :] = jnp.zeros((CONV_HALO, D_CONV), F32)

    ce_ref[CONV_HALO:CONV_HALO + ts, :] = ca_ref[...].astype(F32) * _sigmoid(cg_ref[...].astype(F32))
    for r0 in range(0, ts, CONV_ROWS):
        acc = jnp.broadcast_to(b_ref[...], (CONV_ROWS, D_CONV))
        for j in range(CONV_WIDTH):
            off = CONV_HALO + r0 - (CONV_WIDTH - 1) + j
            acc = acc + w_ref[j:j + 1, :] * ce_ref[off:off + CONV_ROWS, :]
        mu = jnp.mean(acc, axis=-1, keepdims=True)
        xc = acc - mu
        y = xc * lax.rsqrt(jnp.mean(xc * xc, axis=-1, keepdims=True) + EPS) * lg_ref[...] + lb_ref[...]
        o_ref[r0:r0 + CONV_ROWS, :] = (y * _sigmoid(y)).astype(o_ref.dtype)
    ce_ref[0:CONV_HALO, :] = ce_ref[ts:ts + CONV_HALO, :]


def _cconv_branch(zmix, batch, seq, w, b, lg, lb):
    nt = seq // SEQ_TILE
    return pl.pallas_call(
        _cconv_kernel,
        grid=(batch, nt),
        in_specs=[
            pl.BlockSpec((SEQ_TILE, D_CONV), lambda b_, t: (b_ * nt + t, COL_CONV_A // D_CONV)),
            pl.BlockSpec((SEQ_TILE, D_CONV), lambda b_, t: (b_ * nt + t, COL_CONV_G // D_CONV)),
            _resident((CONV_WIDTH, D_CONV)),
            _resident((1, D_CONV)),
            _resident((1, D_CONV)),
            _resident((1, D_CONV)),
        ],
        out_specs=pl.BlockSpec((SEQ_TILE, D_CONV), lambda b_, t: (b_ * nt + t, 0)),
        out_shape=jax.ShapeDtypeStruct((batch * seq, D_CONV), BF16),
        scratch_shapes=[pltpu.VMEM((SEQ_TILE + CONV_HALO, D_CONV), F32)],
        compiler_params=_params(("parallel", "arbitrary")),
        name="cconv_branch",
    )(zmix, zmix, w, b, lg, lb)


def _t5_bucket(rel):
    nb = REL_BUCKETS // 2
    ret = jnp.where(rel > 0, nb, 0)
    n = jnp.abs(rel)
    max_exact = nb // 2
    large = max_exact + (jnp.log(jnp.maximum(n, 1).astype(F32) / max_exact)
                         / math.log(REL_MAX_DIST / max_exact) * (nb - max_exact)).astype(jnp.int32)
    large = jnp.minimum(large, nb - 1)
    return ret + jnp.where(n < max_exact, n, large)


def _bias_tiles_kernel(tbl_ref, bucket_ref, o_ref, *, far_bucket):
    h = pl.program_id(0)
    bucket = bucket_ref[...]
    acc = jnp.zeros(bucket.shape, F32)
    for bkt in range(REL_BUCKETS):
        acc = jnp.where(bucket == bkt, tbl_ref[bkt, h], acc)
    acc = acc - tbl_ref[far_bucket, h]
    o_ref[...] = jnp.where(bucket < 0, MASK_VALUE, acc)


def _bias_tiles(rel_bias):
    kpos = jnp.arange(DA_TK, dtype=jnp.int32)[:, None]
    qpos = jnp.arange(DA_TQ, dtype=jnp.int32)[None, :]
    prev = _t5_bucket(kpos - DA_TK - qpos)
    diag = jnp.where(kpos // CHUNK <= qpos // CHUNK, _t5_bucket(kpos - qpos), -1)
    buckets = jnp.stack([jnp.concatenate([prev, prev], axis=1),
                         jnp.concatenate([diag, diag], axis=1)])
    assert DA_TK >= REL_MAX_DIST and DA_TQ == DA_TK
    far_bucket = REL_BUCKETS // 2 - 1
    return pl.pallas_call(
        functools.partial(_bias_tiles_kernel, far_bucket=far_bucket),
        grid=(DA_HEADS,),
        in_specs=[
            pl.BlockSpec(memory_space=pltpu.SMEM),
            pl.BlockSpec((2, DA_TK, 2 * DA_TQ), lambda h: (0, 0, 0)),
        ],
        out_specs=pl.BlockSpec((None, 2, DA_TK, 2 * DA_TQ), lambda h: (h, 0, 0, 0)),
        out_shape=jax.ShapeDtypeStruct((DA_HEADS, 2, DA_TK, 2 * DA_TQ), F32),
        compiler_params=_params(("parallel",)),
        name="bias_tiles",
    )(rel_bias, buckets)


def _da_kernel(q_ref, k_ref, v_ref, bias_ref, lam_ref, sg_ref, o_ref, vt_ref, m_ref, l_ref, acc_ref, *, lam_init):
    i = pl.program_id(1)
    tq, tk = DA_TQ, DA_TK
    blk = 2 * DA_DIM
    heads = range(DA_HEADS)

    @pl.when(i == 0)
    def _():
        for h in heads:
            for j in range(vt_ref.shape[1]):
                v = v_ref[j * tk:(j + 1) * tk, h * DA_VDIM:(h + 1) * DA_VDIM]
                vt_ref[h, j] = v.astype(F32).T.astype(BF16)

    lane = lax.broadcasted_iota(jnp.int32, (tq, blk), 1)
    q2 = []
    for h in heads:
        q = q_ref[:, h * blk:(h + 1) * blk] * jnp.asarray(DA_DIM ** -0.5, q_ref.dtype)
        zero = jnp.zeros_like(q)
        q2.append(jnp.concatenate([jnp.where(lane < DA_DIM, q, zero), jnp.where(lane >= DA_DIM, q, zero)],
                                  axis=0))

    m_ref[...] = jnp.full(m_ref.shape, MASK_VALUE, F32)
    l_ref[...] = jnp.zeros(l_ref.shape, F32)
    acc_ref[...] = jnp.zeros(acc_ref.shape, F32)

    def step(j, near):
        k0 = pl.multiple_of(j * tk, tk)

        def logits(h):
            kb = k_ref[pl.ds(k0, tk), h * blk:(h + 1) * blk]
            s = lax.dot_general(kb, q2[h], (((1,), (1,)), ((), ())), preferred_element_type=F32)
            return s if near is None else s + bias_ref[h, near]

        s_next = logits(0)
        for h in heads:
            s = s_next
            if h + 1 < DA_HEADS:
                s_next = logits(h + 1)
            m_prev = m_ref[h]
            m_new = jnp.maximum(m_prev, jnp.max(s, axis=0, keepdims=True))
            alpha = jnp.exp(m_prev - m_new)
            p = jnp.exp(s - m_new)
            l_ref[h] = alpha * l_ref[h] + jnp.sum(p, axis=0, keepdims=True)
            acc_ref[h] = alpha * acc_ref[h] + jnp.dot(vt_ref[h, j], p.astype(BF16), preferred_element_type=F32)
            m_ref[h] = m_new

    def far_body(j, c):
        step(j, None)
        return c

    lax.fori_loop(0, jnp.maximum(i - 1, 0), far_body, 0)

    @pl.when(i >= 1)
    def _():
        step(i - 1, 0)

    step(i, 1)

    lv = lam_ref[...]
    lam = (jnp.exp(jnp.sum(lv[0:1, :] * lv[1:2, :], axis=-1, keepdims=True))
           - jnp.exp(jnp.sum(lv[2:3, :] * lv[3:4, :], axis=-1, keepdims=True)) + lam_init)
    for h in heads:
        o = acc_ref[h] / l_ref[h]
        d = o[:, :tq] - lam * o[:, tq:]
        y = d * lax.rsqrt(jnp.mean(d * d, axis=0, keepdims=True) + EPS)
        o_ref[:, h * DA_VDIM:(h + 1) * DA_VDIM] = (y.T * sg_ref[...] * (1.0 - lam_init)).astype(o_ref.dtype)


def _diff_attention(zmix, batch, seq, bias_tiles, lam_vec, subln_g, lam_init):
    nq = seq // DA_TQ
    assert COL_Q % D_DA == 0 and COL_K % D_DA == 0 and COL_V % D_DA == 0
    return pl.pallas_call(
        functools.partial(_da_kernel, lam_init=lam_init),
        grid=(batch, nq),
        in_specs=[
            pl.BlockSpec((DA_TQ, D_DA), lambda b, i: (b * nq + i, COL_Q // D_DA)),
            pl.BlockSpec((seq, D_DA), lambda b, i: (b, COL_K // D_DA)),
            pl.BlockSpec((seq, D_DA), lambda b, i: (b, COL_V // D_DA)),
            _resident((DA_HEADS, 2, DA_TK, 2 * DA_TQ)),
            _resident((4, DA_DIM)),
            _resident((1, DA_VDIM)),
        ],
        out_specs=pl.BlockSpec((DA_TQ, D_DA), lambda b, i: (b * nq + i, 0)),
        out_shape=jax.ShapeDtypeStruct((batch * seq, D_DA), BF16),
        scratch_shapes=[
            pltpu.VMEM((DA_HEADS, seq // DA_TK, DA_VDIM, DA_TK), BF16),
            pltpu.VMEM((DA_HEADS, 1, 2 * DA_TQ), F32),
            pltpu.VMEM((DA_HEADS, 1, 2 * DA_TQ), F32),
            pltpu.VMEM((DA_HEADS, DA_VDIM, 2 * DA_TQ), F32),
        ],
        compiler_params=_params(("parallel", "arbitrary")),
        name="diff_attention",
    )(zmix, zmix, zmix, bias_tiles, lam_vec, subln_g)


def _merge_kernel(x_ref, ya_ref, yb_ref, yc_ref, g0_ref, g1_ref, g2_ref,
                  wa_ref, wb_ref, wc_ref, wo_ref, o_ref):
    merged = (g0_ref[...].astype(F32) * jnp.dot(ya_ref[...], wa_ref[...], preferred_element_type=F32)
              + g1_ref[...].astype(F32) * jnp.dot(yb_ref[...], wb_ref[...], preferred_element_type=F32)
              + g2_ref[...].astype(F32) * jnp.dot(yc_ref[...], wc_ref[...], preferred_element_type=F32))
    o_ref[...] = x_ref[...] + jnp.dot(merged.astype(BF16), wo_ref[...], preferred_element_type=F32)


def _merge(x2d, ya, yb, yc, zmix, wa, wb, wc, wo):
    n = x2d.shape[0]
    row = lambda i: (i, 0)
    return pl.pallas_call(
        _merge_kernel,
        grid=(n // ROW_TILE,),
        in_specs=[
            pl.BlockSpec((ROW_TILE, D_MODEL), row),
            pl.BlockSpec((ROW_TILE, D_LRU), row),
            pl.BlockSpec((ROW_TILE, D_CONV), row),
            pl.BlockSpec((ROW_TILE, D_DA), row),
            pl.BlockSpec((ROW_TILE, D_MODEL), lambda i: (i, 0)),
            pl.BlockSpec((ROW_TILE, D_MODEL), lambda i: (i, 1)),
            pl.BlockSpec((ROW_TILE, D_MODEL), lambda i: (i, 2)),
            _resident((D_LRU, D_MODEL)),
            _resident((D_CONV, D_MODEL)),
            _resident((D_DA, D_MODEL)),
            _resident((D_MODEL, D_MODEL)),
        ],
        out_specs=pl.BlockSpec((ROW_TILE, D_MODEL), row),
        out_shape=jax.ShapeDtypeStruct((n, D_MODEL), F32),
        compiler_params=_params(("parallel",)),
        name="merge",
    )(x2d, ya, yb, yc, zmix, zmix, zmix, wa, wb, wc, wo)


def _xattn_kernel(x_ref, g_ref, wq_ref, kv_ref, wo_ref, o_ref, oc_ref):
    x = x_ref[...]
    hq = _rms_norm(x, g_ref[...]).astype(BF16)
    q = (jnp.dot(hq, wq_ref[...], preferred_element_type=F32) * (XA_DIM ** -0.5)).astype(BF16)
    for h in range(XA_HEADS):
        c0 = h * XA_DIM
        kh = kv_ref[:, c0:c0 + XA_DIM]
        vh = kv_ref[:, D_MODEL + c0:D_MODEL + c0 + XA_DIM]
        s = lax.dot_general(q[:, c0:c0 + XA_DIM], kh, (((1,), (1,)), ((), ())), preferred_element_type=F32)
        p = jnp.exp(s - jnp.max(s, axis=-1, keepdims=True))
        l = jnp.sum(p, axis=-1, keepdims=True)
        o = jnp.dot(p.astype(BF16), vh, preferred_element_type=F32) / l
        oc_ref[:, c0:c0 + XA_DIM] = o.astype(BF16)
    o_ref[...] = x + jnp.dot(oc_ref[...], wo_ref[...], preferred_element_type=F32)


def _cross_attention(x2d, batch, seq, g, wq, kv, wo):
    nt = seq // ROW_TILE
    n_mem = kv.shape[0] // batch
    return pl.pallas_call(
        _xattn_kernel,
        grid=(batch, nt),
        in_specs=[
            pl.BlockSpec((ROW_TILE, D_MODEL), lambda b, t: (b * nt + t, 0)),
            _resident((1, D_MODEL)),
            _resident((D_MODEL, D_MODEL)),
            pl.BlockSpec((n_mem, 2 * D_MODEL), lambda b, t: (b, 0)),
            _resident((D_MODEL, D_MODEL)),
        ],
        out_specs=pl.BlockSpec((ROW_TILE, D_MODEL), lambda b, t: (b * nt + t, 0)),
        out_shape=jax.ShapeDtypeStruct(x2d.shape, F32),
        scratch_shapes=[pltpu.VMEM((ROW_TILE, D_MODEL), BF16)],
        compiler_params=_params(("parallel", "arbitrary")),
        name="cross_attention",
    )(x2d, g, wq, kv, wo)


def _ffn_kernel(x_ref, g_ref, w1_ref, w3_ref, cw_ref, cb_ref, w2_ref, fg_ref, o_ref,
                ae_ref, halo_ref, hm_ref, *, final_norm):
    ts = x_ref.shape[0]
    halo = SUBLANES

    @pl.when(pl.program_id(1) == 0)
    def _():
        halo_ref[...] = jnp.zeros(halo_ref.shape, F32)

    x = x_ref[...]
    hf = _rms_norm(x, g_ref[...]).astype(BF16)
    for c0 in range(0, D_FF, FFN_COL_CHUNK):
        cs = slice(c0, c0 + FFN_COL_CHUNK)
        a = jnp.dot(hf, w1_ref[:, cs], preferred_element_type=F32)
        ae_ref[0:halo, :] = halo_ref[:, cs]
        ae_ref[halo:halo + ts, :] = a
        halo_ref[:, cs] = a[ts - halo:ts, :]
        y = cb_ref[:, cs] + cw_ref[FFN_CONV - 1:FFN_CONV, cs] * a
        for j in range(FFN_CONV - 1):
            off = halo - (FFN_CONV - 1) + j
            y = y + cw_ref[j:j + 1, cs] * ae_ref[off:off + ts, :]
        up = jnp.dot(hf, w3_ref[:, cs], preferred_element_type=F32)
        hm_ref[:, cs] = (y * _sigmoid(y) * up).astype(BF16)
    out = x + jnp.dot(hm_ref[...], w2_ref[...], preferred_element_type=F32)
    if final_norm:
        out = _rms_norm(out, fg_ref[...])
    o_ref[...] = out


def _ffn(x2d, batch, seq, g, w1, w3, cw, cb, w2, fg, final_norm):
    nt = seq // SEQ_TILE
    return pl.pallas_call(
        functools.partial(_ffn_kernel, final_norm=final_norm),
        grid=(batch, nt),
        in_specs=[
            pl.BlockSpec((SEQ_TILE, D_MODEL), lambda b, t: (b * nt + t, 0)),
            _resident((1, D_MODEL)),
            _resident((D_MODEL, D_FF)),
            _resident((D_MODEL, D_FF)),
            _resident((FFN_CONV, D_FF)),
            _resident((1, D_FF)),
            _resident((D_FF, D_MODEL)),
            _resident((1, D_MODEL)),
        ],
        out_specs=pl.BlockSpec((SEQ_TILE, D_MODEL), lambda b, t: (b * nt + t, 0)),
        out_shape=jax.ShapeDtypeStruct(x2d.shape, F32),
        scratch_shapes=[
            pltpu.VMEM((SEQ_TILE + SUBLANES, FFN_COL_CHUNK), F32),
            pltpu.VMEM((SUBLANES, D_FF), F32),
            pltpu.VMEM((SEQ_TILE, D_FF), BF16),
        ],
        compiler_params=_params(("parallel", "arbitrary")),
        name="ffn_final" if final_norm else "ffn",
    )(x2d, g, w1, w3, cw, cb, w2, fg)


def _block_diag(w):
    nb, bi, bj = w.shape
    eye = jnp.eye(nb, dtype=w.dtype)
    return jnp.einsum('hij,hg->higj', w, eye).reshape(nb * bi, nb * bj)


def kernel(x, mem, rel_bias, norm_mix_g, w_in, w_gate, b_gate, lru_conv_w, lru_conv_b, lru_wr, lru_br, lru_wi, lru_bi, lru_lambda, lru_out, cm_conv_w, cm_conv_b, cm_ln_g, cm_ln_b, cm_out, da_lambda, da_subln_g, da_out, w_o, norm_xa_g, norm_mem_g, xa_wq, xa_wkv, xa_wo, norm_ffn_g, ffn_w1, ffn_w3, ffn_conv_w, ffn_conv_b, ffn_w2, final_g):
    batch, seq, d = x.shape
    n_mem = mem.shape[1]
    depth = w_in.shape[0]
    assert d == D_MODEL and seq % SEQ_TILE == 0 and seq % ROW_TILE == 0 and seq % DA_TQ == 0

    bias_tiles = _bias_tiles(rel_bias)
    x2d = x.reshape(batch * seq, d)
    mem2d = mem.reshape(batch * n_mem, d)
    row = lambda v: v.reshape(1, -1)
    no_bias = jnp.zeros((1, MIX_COL_CHUNK), F32)

    for l in range(depth):
        w_mix = jnp.concatenate([w_gate[l, 0], w_gate[l, 1], w_gate[l, 2], w_in[l]], axis=1).astype(BF16)
        zmix = _norm_proj(x2d, row(norm_mix_g[l]), w_mix, row(b_gate[l]), N_GATE, "mix_proj")
        wri = jnp.concatenate([_block_diag(lru_wr[l]), _block_diag(lru_wi[l])], axis=1).astype(BF16)
        bri = jnp.concatenate([lru_br[l], lru_bi[l]]).reshape(1, -1)
        ya = _lru_branch(zmix, batch, seq, lru_conv_w[l], row(lru_conv_b[l]), wri, bri, row(lru_lambda[l]))
        yb = _cconv_branch(zmix, batch, seq, cm_conv_w[l], row(cm_conv_b[l]), row(cm_ln_g[l]), row(cm_ln_b[l]))
        lam_init = 0.8 - 0.6 * math.exp(-0.3 * l)
        yc = _diff_attention(zmix, batch, seq, bias_tiles, da_lambda[l], row(da_subln_g[l]), lam_init)
        x2d = _merge(x2d, ya, yb, yc, zmix, lru_out[l].astype(BF16), cm_out[l].astype(BF16),
                     da_out[l].astype(BF16), w_o[l].astype(BF16))
        kv = _norm_proj(mem2d, row(norm_mem_g[l]), xa_wkv[l].astype(BF16), no_bias, 0, "mem_kv")
        x2d = _cross_attention(x2d, batch, seq, row(norm_xa_g[l]), xa_wq[l].astype(BF16), kv,
                               xa_wo[l].astype(BF16))
        x2d = _ffn(x2d, batch, seq, row(norm_ffn_g[l]), ffn_w1[l].astype(BF16), ffn_w3[l].astype(BF16),
                   ffn_conv_w[l], row(ffn_conv_b[l]), ffn_w2[l].astype(BF16), row(final_g),
                   final_norm=(l == depth - 1))
    return x2d.reshape(batch, seq, d)
```

```python
import functools
import math

import jax
import jax.numpy as jnp
import numpy as np
from jax import lax
from jax.experimental import pallas as pl
from jax.experimental.pallas import tpu as pltpu

F32 = jnp.float32
BF16 = jnp.bfloat16

D_MODEL = 1024
CHUNK = 64
D_LRU = 512
LRU_BLOCKS = 8
LRU_CONV = 4
LRU_C = 8.0
D_CONV = 512
CONV_WIDTH = 31
DA_HEADS = 4
DA_DIM = 64
DA_VDIM = 2 * DA_DIM
D_DA = DA_HEADS * DA_VDIM
REL_BUCKETS = 32
REL_MAX_DIST = 128
XA_HEADS = 4
XA_DIM = D_MODEL // XA_HEADS
D_FF = 2816
FFN_CONV = 3
N_BRANCH = 3
EPS = 1e-6

N_GATE = N_BRANCH * D_MODEL
COL_LRU_X = N_GATE
COL_LRU_G = COL_LRU_X + D_LRU
COL_CONV_A = COL_LRU_G + D_LRU
COL_CONV_G = COL_CONV_A + D_CONV
COL_Q = COL_CONV_G + D_CONV
COL_K = COL_Q + DA_HEADS * 2 * DA_DIM
COL_V = COL_K + DA_HEADS * 2 * DA_DIM
N_MIX = COL_V + D_DA

SUBLANES = 8
LANES = 128
VMEM_LIMIT = 56 * 1024 * 1024
ROW_TILE = 512
MIX_COL_CHUNK = 512
SEQ_TILE = 512
CONV_ROWS = 32
CONV_HALO = 32
DA_TQ = 256
DA_TK = 256
FFN_COL_CHUNK = 256
DA_ONES_ROWS = 16
MASK_VALUE = -1e30
LOG2E = math.log2(math.e)
DA_Q_SCALE = DA_DIM ** -0.5 * LOG2E


def _resident(shape):
    nd = len(shape)
    return pl.BlockSpec(shape, lambda *_: (0,) * nd, pipeline_mode=pl.Buffered(1))


def _params(semantics):
    return pltpu.CompilerParams(dimension_semantics=semantics, vmem_limit_bytes=VMEM_LIMIT)


def _rms_norm(x, g):
    return x * lax.rsqrt(jnp.mean(x * x, axis=-1, keepdims=True) + EPS) * g


def _sigmoid(x):
    return 1.0 / (1.0 + jnp.exp(-x))


def _gelu_tanh(x):
    c = math.sqrt(2.0 / math.pi)
    return 0.5 * x * (1.0 + jnp.tanh(c * (x + 0.044715 * (x * x * x))))


def _norm_proj_kernel(x_ref, g_ref, w_ref, b_ref, o_ref, *, n_sig, chunk, scaled):
    h = _rms_norm(x_ref[...], g_ref[...]).astype(BF16)
    n_out = o_ref.shape[1]
    for c0 in range(0, n_out, chunk):
        y = jnp.dot(h, w_ref[:, c0:c0 + chunk], preferred_element_type=F32)
        if c0 < n_sig:
            y = _sigmoid(y + b_ref[:, c0:c0 + chunk])
        if scaled is not None and scaled[0] <= c0 < scaled[1]:
            y = y * scaled[2]
        o_ref[:, c0:c0 + chunk] = y.astype(o_ref.dtype)


def _norm_proj(x2d, g, w, b, n_sig, name, scaled=None):
    n, d = x2d.shape
    n_out = w.shape[1]
    assert n % ROW_TILE == 0 and n_out % MIX_COL_CHUNK == 0 and n_sig % MIX_COL_CHUNK == 0
    assert scaled is None or (scaled[0] % MIX_COL_CHUNK == 0 and scaled[1] % MIX_COL_CHUNK == 0)
    return pl.pallas_call(
        functools.partial(_norm_proj_kernel, n_sig=n_sig, chunk=MIX_COL_CHUNK, scaled=scaled),
        grid=(n // ROW_TILE,),
        in_specs=[
            pl.BlockSpec((ROW_TILE, d), lambda i: (i, 0)),
            _resident((1, d)),
            _resident((d, n_out)),
            _resident((1, b.shape[1])),
        ],
        out_specs=pl.BlockSpec((ROW_TILE, n_out), lambda i: (i, 0)),
        out_shape=jax.ShapeDtypeStruct((n, n_out), BF16),
        compiler_params=_params(("parallel",)),
        name=name,
    )(x2d, g, w, b)


def _lru_kernel(zx_ref, zg_ref, cw_ref, cb_ref, wri_ref, bri_ref, lam_ref, o_ref,
                xe_ref, a_ref, u_ref, carry_ref):
    ts = zx_ref.shape[0]
    halo = SUBLANES

    @pl.when(pl.program_id(1) == 0)
    def _():
        xe_ref[0:halo, :] = jnp.zeros((halo, D_LRU), F32)
        carry_ref[...] = jnp.zeros((SUBLANES, D_LRU), F32)

    xe_ref[halo:halo + ts, :] = zx_ref[...].astype(F32)
    xa = cb_ref[...] + cw_ref[LRU_CONV - 1:LRU_CONV, :] * xe_ref[halo:halo + ts, :]
    for j in range(LRU_CONV - 1):
        off = halo - (LRU_CONV - 1) + j
        xa = xa + cw_ref[j:j + 1, :] * xe_ref[off:off + ts, :]
    xe_ref[0:halo, :] = xe_ref[ts:ts + halo, :]

    ri = _sigmoid(jnp.dot(xa.astype(BF16), wri_ref[...], preferred_element_type=F32) + bri_ref[...])
    r = ri[:, :D_LRU]
    gi = ri[:, D_LRU:]
    nlam = -lam_ref[...]
    softplus = jnp.maximum(nlam, 0.0) + jnp.log(1.0 + jnp.exp(-jnp.abs(nlam)))
    log_a = (-LRU_C) * r * softplus
    a = jnp.exp(log_a)
    u = jnp.sqrt(1.0 - jnp.exp(2.0 * log_a)) * (gi * xa)

    row = lax.broadcasted_iota(jnp.int32, (ts, D_LRU), 0) & (SUBLANES - 1)
    s = 1
    while s < SUBLANES:
        keep = row >= s
        u = jnp.where(keep, a * pltpu.roll(u, s, 0) + u, u)
        a = jnp.where(keep, a * pltpu.roll(a, s, 0), a)
        s *= 2
    a_ref[...] = a
    u_ref[...] = u

    carry = carry_ref[...]
    for g0 in range(0, ts, SUBLANES):
        h = a_ref[g0:g0 + SUBLANES, :] * carry + u_ref[g0:g0 + SUBLANES, :]
        u_ref[g0:g0 + SUBLANES, :] = h
        carry = jnp.broadcast_to(h[SUBLANES - 1:SUBLANES, :], (SUBLANES, D_LRU))
    carry_ref[...] = carry

    o_ref[...] = (u_ref[...] * _gelu_tanh(zg_ref[...].astype(F32))).astype(o_ref.dtype)


def _lru_branch(zmix, batch, seq, cw, cb, wri, bri, lam):
    nt = seq // SEQ_TILE
    return pl.pallas_call(
        _lru_kernel,
        grid=(batch, nt),
        in_specs=[
            pl.BlockSpec((SEQ_TILE, D_LRU), lambda b, t: (b * nt + t, COL_LRU_X // D_LRU)),
            pl.BlockSpec((SEQ_TILE, D_LRU), lambda b, t: (b * nt + t, COL_LRU_G // D_LRU)),
            _resident((LRU_CONV, D_LRU)),
            _resident((1, D_LRU)),
            _resident((D_LRU, 2 * D_LRU)),
            _resident((1, 2 * D_LRU)),
            _resident((1, D_LRU)),
        ],
        out_specs=pl.BlockSpec((SEQ_TILE, D_LRU), lambda b, t: (b * nt + t, 0)),
        out_shape=jax.ShapeDtypeStruct((batch * seq, D_LRU), BF16),
        scratch_shapes=[
            pltpu.VMEM((SEQ_TILE + SUBLANES, D_LRU), F32),
            pltpu.VMEM((SEQ_TILE, D_LRU), F32),
            pltpu.VMEM((SEQ_TILE, D_LRU), F32),
            pltpu.VMEM((SUBLANES, D_LRU), F32),
        ],
        compiler_params=_params(("parallel", "arbitrary")),
        name="lru_branch",
    )(zmix, zmix, cw, cb, wri, bri, lam)


def _cconv_kernel(ca_ref, cg_ref, w_ref, b_ref, lg_ref, lb_ref, o_ref, ce_ref, sh_ref, wb_ref):
    ts = ca_ref.shape[0]
    first_tap = CONV_HALO - (CONV_WIDTH - 1)
    sh_rows = sh_ref.shape[1]

    @pl.when(pl.program_id(1) == 0)
    def _():
        ce_ref[0:CONV_HALO, :] = jnp.zeros((CONV_HALO, D_CONV), F32)
        for j in range(CONV_WIDTH):
            wb_ref[j] = jnp.broadcast_to(w_ref[j:j + 1, :], (SUBLANES, D_CONV))

    ce_ref[CONV_HALO:CONV_HALO + ts, :] = ca_ref[...].astype(F32) * _sigmoid(cg_ref[...].astype(F32))
    for r in range(1, SUBLANES):
        sh_ref[r - 1] = ce_ref[r:r + sh_rows, :]
    groups = range(0, CONV_ROWS, SUBLANES)
    for r0 in range(0, ts, CONV_ROWS):
        accs = [jnp.broadcast_to(b_ref[...], (SUBLANES, D_CONV)) for _ in groups]
        for j in range(CONV_WIDTH):
            r = (first_tap + j) % SUBLANES
            base = r0 + first_tap + j - r
            wb = wb_ref[j]
            for k, g0 in enumerate(groups):
                lo = base + g0
                src = ce_ref[lo:lo + SUBLANES, :] if r == 0 else sh_ref[r - 1, lo:lo + SUBLANES, :]
                accs[k] = accs[k] + wb * src
        acc = jnp.concatenate(accs, axis=0)
        mu = jnp.mean(acc, axis=-1, keepdims=True)
        xc = acc - mu
        y = xc * lax.rsqrt(jnp.mean(xc * xc, axis=-1, keepdims=True) + EPS) * lg_ref[...] + lb_ref[...]
        o_ref[r0:r0 + CONV_ROWS, :] = (y * _sigmoid(y)).astype(o_ref.dtype)
    ce_ref[0:CONV_HALO, :] = ce_ref[ts:ts + CONV_HALO, :]


def _cconv_branch(zmix, batch, seq, w, b, lg, lb):
    nt = seq // SEQ_TILE
    return pl.pallas_call(
        _cconv_kernel,
        grid=(batch, nt),
        in_specs=[
            pl.BlockSpec((SEQ_TILE, D_CONV), lambda b_, t: (b_ * nt + t, COL_CONV_A // D_CONV)),
            pl.BlockSpec((SEQ_TILE, D_CONV), lambda b_, t: (b_ * nt + t, COL_CONV_G // D_CONV)),
            _resident((CONV_WIDTH, D_CONV)),
            _resident((1, D_CONV)),
            _resident((1, D_CONV)),
            _resident((1, D_CONV)),
        ],
        out_specs=pl.BlockSpec((SEQ_TILE, D_CONV), lambda b_, t: (b_ * nt + t, 0)),
        out_shape=jax.ShapeDtypeStruct((batch * seq, D_CONV), BF16),
        scratch_shapes=[
            pltpu.VMEM((SEQ_TILE + CONV_HALO, D_CONV), F32),
            pltpu.VMEM((SUBLANES - 1, SEQ_TILE + CONV_HALO - SUBLANES, D_CONV), F32),
            pltpu.VMEM((CONV_WIDTH, SUBLANES, D_CONV), F32),
        ],
        compiler_params=_params(("parallel", "arbitrary")),
        name="cconv_branch",
    )(zmix, zmix, w, b, lg, lb)


def _t5_bucket(rel):
    nb = REL_BUCKETS // 2
    ret = jnp.where(rel > 0, nb, 0)
    n = jnp.abs(rel)
    max_exact = nb // 2
    large = max_exact + (jnp.log(jnp.maximum(n, 1).astype(F32) / max_exact)
                         / math.log(REL_MAX_DIST / max_exact) * (nb - max_exact)).astype(jnp.int32)
    large = jnp.minimum(large, nb - 1)
    return ret + jnp.where(n < max_exact, n, large)


def _bias_tiles_kernel(tbl_ref, bucket_ref, o_ref, *, far_bucket):
    h = pl.program_id(0)
    bucket = bucket_ref[...]
    acc = jnp.zeros(bucket.shape, F32)
    for bkt in range(REL_BUCKETS):
        acc = jnp.where(bucket == bkt, tbl_ref[bkt, h], acc)
    acc = (acc - tbl_ref[far_bucket, h]) * LOG2E
    o_ref[...] = jnp.where(bucket < 0, MASK_VALUE, acc)


def _bias_tiles(rel_bias):
    kpos = jnp.arange(DA_TK, dtype=jnp.int32)[:, None]
    qpos = jnp.arange(DA_TQ, dtype=jnp.int32)[None, :]
    prev = _t5_bucket(kpos - DA_TK - qpos)
    diag = jnp.where(kpos // CHUNK <= qpos // CHUNK, _t5_bucket(kpos - qpos), -1)
    buckets = jnp.stack([prev, diag])
    assert DA_TK >= REL_MAX_DIST and DA_TQ == DA_TK
    far_bucket = REL_BUCKETS // 2 - 1
    return pl.pallas_call(
        functools.partial(_bias_tiles_kernel, far_bucket=far_bucket),
        grid=(DA_HEADS,),
        in_specs=[
            pl.BlockSpec(memory_space=pltpu.SMEM),
            pl.BlockSpec((2, DA_TK, DA_TQ), lambda h: (0, 0, 0)),
        ],
        out_specs=pl.BlockSpec((None, 2, DA_TK, DA_TQ), lambda h: (h, 0, 0, 0)),
        out_shape=jax.ShapeDtypeStruct((DA_HEADS, 2, DA_TK, DA_TQ), F32),
        compiler_params=_params(("parallel",)),
        name="bias_tiles",
    )(rel_bias, buckets)


def _da_kernel(q_ref, k_ref, v_ref, bias_ref, lam_ref, sg_ref, o_ref,
               vt_ref, m_ref, l_ref, acc_ref, s_ref, *, lam_init):
    i = pl.program_id(1)
    tq, tk = DA_TQ, DA_TK
    blk = 2 * DA_DIM
    chains = [(h, c) for h in range(DA_HEADS) for c in range(2)]

    @pl.when(i == 0)
    def _():
        ones = jnp.ones((DA_ONES_ROWS, tk), BF16)
        for h in range(DA_HEADS):
            for j in range(vt_ref.shape[1]):
                v = v_ref[j * tk:(j + 1) * tk, h * DA_VDIM:(h + 1) * DA_VDIM]
                vt_ref[h, j, 0:DA_VDIM, :] = v.astype(F32).T.astype(BF16)
                vt_ref[h, j, DA_VDIM:DA_VDIM + DA_ONES_ROWS, :] = ones

    lane = lax.broadcasted_iota(jnp.int32, (tq, blk), 1)
    qz = []
    for h, c in chains:
        q = q_ref[:, h * blk:(h + 1) * blk]
        qz.append(jnp.where((lane >= DA_DIM) == (c == 1), q, jnp.zeros_like(q)))

    m_ref[...] = jnp.full(m_ref.shape, MASK_VALUE, F32)
    l_ref[...] = jnp.zeros(l_ref.shape, F32)
    acc_ref[...] = jnp.zeros(acc_ref.shape, F32)

    def step(j, near):
        k0 = pl.multiple_of(j * tk, tk)

        def logits(n):
            h = chains[n][0]
            kb = k_ref[pl.ds(k0, tk), h * blk:(h + 1) * blk]
            s = lax.dot_general(kb, qz[n], (((1,), (1,)), ((), ())), preferred_element_type=F32)
            return s if near is None else s + bias_ref[h, near]

        block_max = []
        for n in range(len(chains)):
            s = logits(n)
            s_ref[n] = s
            block_max.append(jnp.max(s, axis=0, keepdims=True))
        alphas, probs = [], []
        for n in range(len(chains)):
            m_prev = m_ref[n]
            m_new = jnp.maximum(m_prev, block_max[n])
            alphas.append(jnp.exp2(m_prev - m_new))
            probs.append(jnp.exp2(s_ref[n] - m_new).astype(BF16))
            m_ref[n] = m_new
        for n, (h, c) in enumerate(chains):
            pv = jnp.dot(vt_ref[h, j], probs[n], preferred_element_type=F32)
            acc_ref[n] = alphas[n] * acc_ref[n] + pv[0:DA_VDIM, :]
            l_ref[n] = alphas[n] * l_ref[n] + pv[DA_VDIM:DA_VDIM + 1, :]

    def far_body(j, carry):
        step(j, None)
        return carry

    lax.fori_loop(0, jnp.maximum(i - 1, 0), far_body, 0)

    @pl.when(i >= 1)
    def _():
        step(i - 1, 0)

    step(i, 1)

    lv = lam_ref[...]
    lam = (jnp.exp(jnp.sum(lv[0:1, :] * lv[1:2, :], axis=-1, keepdims=True))
           - jnp.exp(jnp.sum(lv[2:3, :] * lv[3:4, :], axis=-1, keepdims=True)) + lam_init)
    for h in range(DA_HEADS):
        d = acc_ref[2 * h] / l_ref[2 * h] - lam * (acc_ref[2 * h + 1] / l_ref[2 * h + 1])
        y = d * lax.rsqrt(jnp.mean(d * d, axis=0, keepdims=True) + EPS)
        o_ref[:, h * DA_VDIM:(h + 1) * DA_VDIM] = (y.T * sg_ref[...] * (1.0 - lam_init)).astype(o_ref.dtype)


def _diff_attention(zmix, batch, seq, bias_tiles, lam_vec, subln_g, lam_init):
    nq = seq // DA_TQ
    assert COL_Q % D_DA == 0 and COL_K % D_DA == 0 and COL_V % D_DA == 0
    return pl.pallas_call(
        functools.partial(_da_kernel, lam_init=lam_init),
        grid=(batch, nq),
        in_specs=[
            pl.BlockSpec((DA_TQ, D_DA), lambda b, i: (b * nq + i, COL_Q // D_DA)),
            pl.BlockSpec((seq, D_DA), lambda b, i: (b, COL_K // D_DA)),
            pl.BlockSpec((seq, D_DA), lambda b, i: (b, COL_V // D_DA)),
            _resident((DA_HEADS, 2, DA_TK, DA_TQ)),
            _resident((4, DA_DIM)),
            _resident((1, DA_VDIM)),
        ],
        out_specs=pl.BlockSpec((DA_TQ, D_DA), lambda b, i: (b * nq + i, 0)),
        out_shape=jax.ShapeDtypeStruct((batch * seq, D_DA), BF16),
        scratch_shapes=[
            pltpu.VMEM((DA_HEADS, seq // DA_TK, DA_VDIM + DA_ONES_ROWS, DA_TK), BF16),
            pltpu.VMEM((2 * DA_HEADS, 1, DA_TQ), F32),
            pltpu.VMEM((2 * DA_HEADS, 1, DA_TQ), F32),
            pltpu.VMEM((2 * DA_HEADS, DA_VDIM, DA_TQ), F32),
            pltpu.VMEM((2 * DA_HEADS, DA_TK, DA_TQ), F32),
        ],
        compiler_params=_params(("parallel", "arbitrary")),
        name="diff_attention",
    )(zmix, zmix, zmix, bias_tiles, lam_vec, subln_g)


def _merge_kernel(x_ref, ya_ref, yb_ref, yc_ref, g0_ref, g1_ref, g2_ref,
                  wa_ref, wb_ref, wc_ref, wo_ref, o_ref):
    merged = (g0_ref[...].astype(F32) * jnp.dot(ya_ref[...], wa_ref[...], preferred_element_type=F32)
              + g1_ref[...].astype(F32) * jnp.dot(yb_ref[...], wb_ref[...], preferred_element_type=F32)
              + g2_ref[...].astype(F32) * jnp.dot(yc_ref[...], wc_ref[...], preferred_element_type=F32))
    o_ref[...] = x_ref[...] + jnp.dot(merged.astype(BF16), wo_ref[...], preferred_element_type=F32)


def _merge(x2d, ya, yb, yc, zmix, wa, wb, wc, wo):
    n = x2d.shape[0]
    row = lambda i: (i, 0)
    return pl.pallas_call(
        _merge_kernel,
        grid=(n // ROW_TILE,),
        in_specs=[
            pl.BlockSpec((ROW_TILE, D_MODEL), row),
            pl.BlockSpec((ROW_TILE, D_LRU), row),
            pl.BlockSpec((ROW_TILE, D_CONV), row),
            pl.BlockSpec((ROW_TILE, D_DA), row),
            pl.BlockSpec((ROW_TILE, D_MODEL), lambda i: (i, 0)),
            pl.BlockSpec((ROW_TILE, D_MODEL), lambda i: (i, 1)),
            pl.BlockSpec((ROW_TILE, D_MODEL), lambda i: (i, 2)),
            _resident((D_LRU, D_MODEL)),
            _resident((D_CONV, D_MODEL)),
            _resident((D_DA, D_MODEL)),
            _resident((D_MODEL, D_MODEL)),
        ],
        out_specs=pl.BlockSpec((ROW_TILE, D_MODEL), row),
        out_shape=jax.ShapeDtypeStruct((n, D_MODEL), F32),
        compiler_params=_params(("parallel",)),
        name="merge",
    )(x2d, ya, yb, yc, zmix, zmix, zmix, wa, wb, wc, wo)


def _xattn_kernel(x_ref, g_ref, wq_ref, kv_ref, wo_ref, o_ref, oc_ref):
    x = x_ref[...]
    hq = _rms_norm(x, g_ref[...]).astype(BF16)
    q = (jnp.dot(hq, wq_ref[...], preferred_element_type=F32) * (XA_DIM ** -0.5)).astype(BF16)
    for h in range(XA_HEADS):
        c0 = h * XA_DIM
        kh = kv_ref[:, c0:c0 + XA_DIM]
        vh = kv_ref[:, D_MODEL + c0:D_MODEL + c0 + XA_DIM]
        s = lax.dot_general(q[:, c0:c0 + XA_DIM], kh, (((1,), (1,)), ((), ())), preferred_element_type=F32)
        p = jnp.exp(s - jnp.max(s, axis=-1, keepdims=True))
        l = jnp.sum(p, axis=-1, keepdims=True)
        o = jnp.dot(p.astype(BF16), vh, preferred_element_type=F32) / l
        oc_ref[:, c0:c0 + XA_DIM] = o.astype(BF16)
    o_ref[...] = x + jnp.dot(oc_ref[...], wo_ref[...], preferred_element_type=F32)


def _cross_attention(x2d, batch, seq, g, wq, kv, wo):
    nt = seq // ROW_TILE
    n_mem = kv.shape[0] // batch
    return pl.pallas_call(
        _xattn_kernel,
        grid=(batch, nt),
        in_specs=[
            pl.BlockSpec((ROW_TILE, D_MODEL), lambda b, t: (b * nt + t, 0)),
            _resident((1, D_MODEL)),
            _resident((D_MODEL, D_MODEL)),
            pl.BlockSpec((n_mem, 2 * D_MODEL), lambda b, t: (b, 0)),
            _resident((D_MODEL, D_MODEL)),
        ],
        out_specs=pl.BlockSpec((ROW_TILE, D_MODEL), lambda b, t: (b * nt + t, 0)),
        out_shape=jax.ShapeDtypeStruct(x2d.shape, F32),
        scratch_shapes=[pltpu.VMEM((ROW_TILE, D_MODEL), BF16)],
        compiler_params=_params(("parallel", "arbitrary")),
        name="cross_attention",
    )(x2d, g, wq, kv, wo)


def _ffn_kernel(x_ref, g_ref, w1_ref, w3_ref, cw_ref, cb_ref, w2_ref, fg_ref, o_ref,
                ae_ref, halo_ref, hm_ref, *, final_norm):
    ts = x_ref.shape[0]
    halo = SUBLANES

    @pl.when(pl.program_id(1) == 0)
    def _():
        halo_ref[...] = jnp.zeros(halo_ref.shape, F32)

    x = x_ref[...]
    hf = _rms_norm(x, g_ref[...]).astype(BF16)
    for c0 in range(0, D_FF, FFN_COL_CHUNK):
        cs = slice(c0, c0 + FFN_COL_CHUNK)
        a = jnp.dot(hf, w1_ref[:, cs], preferred_element_type=F32)
        ae_ref[0:halo, :] = halo_ref[:, cs]
        ae_ref[halo:halo + ts, :] = a
        halo_ref[:, cs] = a[ts - halo:ts, :]
        y = cb_ref[:, cs] + cw_ref[FFN_CONV - 1:FFN_CONV, cs] * a
        for j in range(FFN_CONV - 1):
            off = halo - (FFN_CONV - 1) + j
            y = y + cw_ref[j:j + 1, cs] * ae_ref[off:off + ts, :]
        up = jnp.dot(hf, w3_ref[:, cs], preferred_element_type=F32)
        hm_ref[:, cs] = (y * _sigmoid(y) * up).astype(BF16)
    out = x + jnp.dot(hm_ref[...], w2_ref[...], preferred_element_type=F32)
    if final_norm:
        out = _rms_norm(out, fg_ref[...])
    o_ref[...] = out


def _ffn(x2d, batch, seq, g, w1, w3, cw, cb, w2, fg, final_norm):
    nt = seq // SEQ_TILE
    return pl.pallas_call(
        functools.partial(_ffn_kernel, final_norm=final_norm),
        grid=(batch, nt),
        in_specs=[
            pl.BlockSpec((SEQ_TILE, D_MODEL), lambda b, t: (b * nt + t, 0)),
            _resident((1, D_MODEL)),
            _resident((D_MODEL, D_FF)),
            _resident((D_MODEL, D_FF)),
            _resident((FFN_CONV, D_FF)),
            _resident((1, D_FF)),
            _resident((D_FF, D_MODEL)),
            _resident((1, D_MODEL)),
        ],
        out_specs=pl.BlockSpec((SEQ_TILE, D_MODEL), lambda b, t: (b * nt + t, 0)),
        out_shape=jax.ShapeDtypeStruct(x2d.shape, F32),
        scratch_shapes=[
            pltpu.VMEM((SEQ_TILE + SUBLANES, FFN_COL_CHUNK), F32),
            pltpu.VMEM((SUBLANES, D_FF), F32),
            pltpu.VMEM((SEQ_TILE, D_FF), BF16),
        ],
        compiler_params=_params(("parallel", "arbitrary")),
        name="ffn_final" if final_norm else "ffn",
    )(x2d, g, w1, w3, cw, cb, w2, fg)


def _block_diag(w):
    nb, bi, bj = w.shape
    eye = jnp.eye(nb, dtype=w.dtype)
    return jnp.einsum('hij,hg->higj', w, eye).reshape(nb * bi, nb * bj)


def kernel(x, mem, rel_bias, norm_mix_g, w_in, w_gate, b_gate, lru_conv_w, lru_conv_b, lru_wr, lru_br, lru_wi, lru_bi, lru_lambda, lru_out, cm_conv_w, cm_conv_b, cm_ln_g, cm_ln_b, cm_out, da_lambda, da_subln_g, da_out, w_o, norm_xa_g, norm_mem_g, xa_wq, xa_wkv, xa_wo, norm_ffn_g, ffn_w1, ffn_w3, ffn_conv_w, ffn_conv_b, ffn_w2, final_g):
    batch, seq, d = x.shape
    n_mem = mem.shape[1]
    depth = w_in.shape[0]
    assert d == D_MODEL and seq % SEQ_TILE == 0 and seq % ROW_TILE == 0 and seq % DA_TQ == 0

    bias_tiles = _bias_tiles(rel_bias)
    x2d = x.reshape(batch * seq, d)
    mem2d = mem.reshape(batch * n_mem, d)
    row = lambda v: v.reshape(1, -1)
    no_bias = jnp.zeros((1, MIX_COL_CHUNK), F32)

    for l in range(depth):
        w_mix = jnp.concatenate([w_gate[l, 0], w_gate[l, 1], w_gate[l, 2], w_in[l]], axis=1).astype(BF16)
        zmix = _norm_proj(x2d, row(norm_mix_g[l]), w_mix, row(b_gate[l]), N_GATE, "mix_proj",
                          scaled=(COL_Q, COL_K, DA_Q_SCALE))
        wri = jnp.concatenate([_block_diag(lru_wr[l]), _block_diag(lru_wi[l])], axis=1).astype(BF16)
        bri = jnp.concatenate([lru_br[l], lru_bi[l]]).reshape(1, -1)
        ya = _lru_branch(zmix, batch, seq, lru_conv_w[l], row(lru_conv_b[l]), wri, bri, row(lru_lambda[l]))
        yb = _cconv_branch(zmix, batch, seq, cm_conv_w[l], row(cm_conv_b[l]), row(cm_ln_g[l]), row(cm_ln_b[l]))
        lam_init = 0.8 - 0.6 * math.exp(-0.3 * l)
        yc = _diff_attention(zmix, batch, seq, bias_tiles, da_lambda[l], row(da_subln_g[l]), lam_init)
        x2d = _merge(x2d, ya, yb, yc, zmix, lru_out[l].astype(BF16), cm_out[l].astype(BF16),
                     da_out[l].astype(BF16), w_o[l].astype(BF16))
        kv = _norm_proj(mem2d, row(norm_mem_g[l]), xa_wkv[l].astype(BF16), no_bias, 0, "mem_kv")
        x2d = _cross_attention(x2d, batch, seq, row(norm_xa_g[l]), xa_wq[l].astype(BF16), kv,
                               xa_wo[l].astype(BF16))
        x2d = _ffn(x2d, batch, seq, row(norm_ffn_g[l]), ffn_w1[l].astype(BF16), ffn_w3[l].astype(BF16),
                   ffn_conv_w[l], row(ffn_conv_b[l]), ffn_w2[l].astype(BF16), row(final_g),
                   final_norm=(l == depth - 1))
    return x2d.reshape(batch, seq, d)
```

```python
import functools
import math

import jax
import jax.numpy as jnp
import numpy as np
from jax import lax
from jax.experimental import pallas as pl
from jax.experimental.pallas import tpu as pltpu

F32 = jnp.float32
BF16 = jnp.bfloat16

D_MODEL = 1024
CHUNK = 64
D_LRU = 512
LRU_BLOCKS = 8
LRU_CONV = 4
LRU_C = 8.0
D_CONV = 512
CONV_WIDTH = 31
DA_HEADS = 4
DA_DIM = 64
DA_VDIM = 2 * DA_DIM
D_DA = DA_HEADS * DA_VDIM
REL_BUCKETS = 32
REL_MAX_DIST = 128
XA_HEADS = 4
XA_DIM = D_MODEL // XA_HEADS
D_FF = 2816
FFN_CONV = 3
N_BRANCH = 3
EPS = 1e-6

IN_LRU_X = 0
IN_LRU_G = IN_LRU_X + D_LRU
IN_CONV_A = IN_LRU_G + D_LRU
IN_CONV_G = IN_CONV_A + D_CONV
IN_Q = IN_CONV_G + D_CONV
IN_K = IN_Q + DA_HEADS * 2 * DA_DIM
IN_V = IN_K + DA_HEADS * 2 * DA_DIM
D_IN = IN_V + D_DA
N_GATE = N_BRANCH * D_MODEL
COL_Q = N_GATE
COL_K = COL_Q + DA_HEADS * 2 * DA_DIM
COL_V = COL_K + DA_HEADS * 2 * DA_DIM
N_MIX = COL_V + D_DA

SUBLANES = 8
LANES = 128
VMEM_LIMIT = 56 * 1024 * 1024
ROW_TILE = 512
MIX_COL_CHUNK = 512
SEQ_TILE = 512
CONV_ROWS = 32
LRU_ROWS = 32
CONV_HALO = 32
DA_TQ = 256
DA_TK = 256
FFN_COL_CHUNK = 256
DA_ONES_ROWS = 16
MASK_VALUE = -1e30
LOG2E = math.log2(math.e)
DA_Q_SCALE = DA_DIM ** -0.5 * LOG2E


def _resident(shape):
    nd = len(shape)
    return pl.BlockSpec(shape, lambda *_: (0,) * nd, pipeline_mode=pl.Buffered(1))


def _params(semantics):
    return pltpu.CompilerParams(dimension_semantics=semantics, vmem_limit_bytes=VMEM_LIMIT)


def _rms_norm(x, g):
    return x * lax.rsqrt(jnp.mean(x * x, axis=-1, keepdims=True) + EPS) * g


def _sigmoid(x):
    return 1.0 / (1.0 + jnp.exp(-x))


def _gelu_tanh(x):
    c = math.sqrt(2.0 / math.pi)
    return 0.5 * x * (1.0 + jnp.tanh(c * (x + 0.044715 * (x * x * x))))


def _norm_proj_kernel(x_ref, g_ref, w_ref, o_ref):
    h = _rms_norm(x_ref[...], g_ref[...]).astype(BF16)
    for c0 in range(0, o_ref.shape[1], MIX_COL_CHUNK):
        y = jnp.dot(h, w_ref[:, c0:c0 + MIX_COL_CHUNK], preferred_element_type=F32)
        o_ref[:, c0:c0 + MIX_COL_CHUNK] = y.astype(o_ref.dtype)


def _norm_proj(x2d, g, w, name):
    n, d = x2d.shape
    n_out = w.shape[1]
    assert n % ROW_TILE == 0 and n_out % MIX_COL_CHUNK == 0
    return pl.pallas_call(
        _norm_proj_kernel,
        grid=(n // ROW_TILE,),
        in_specs=[
            pl.BlockSpec((ROW_TILE, d), lambda i: (i, 0)),
            _resident((1, d)),
            _resident((d, n_out)),
        ],
        out_specs=pl.BlockSpec((ROW_TILE, n_out), lambda i: (i, 0)),
        out_shape=jax.ShapeDtypeStruct((n, n_out), BF16),
        compiler_params=_params(("parallel",)),
        name=name,
    )(x2d, g, w)


def _branch_init(first, ccw_ref, lam_ref, xe_ref, carry_ref, ce_ref, wb_ref, sp_ref):
    @pl.when(first)
    def _():
        xe_ref[0:SUBLANES, :] = jnp.zeros((SUBLANES, D_LRU), F32)
        carry_ref[...] = jnp.zeros((SUBLANES, D_LRU), F32)
        ce_ref[0:CONV_HALO, :] = jnp.zeros((CONV_HALO, D_CONV), F32)
        for j in range(CONV_WIDTH):
            wb_ref[j] = jnp.broadcast_to(ccw_ref[j:j + 1, :], (SUBLANES, D_CONV))
        nlam = -lam_ref[...]
        sp_ref[...] = jnp.maximum(nlam, 0.0) + jnp.log(1.0 + jnp.exp(-jnp.abs(nlam)))


def _lru_gates(cw_ref, cb_ref, wri_ref, bri_ref, xe_ref, xa_ref, ri_ref):
    ts = xa_ref.shape[0]
    halo = SUBLANES
    xa = cb_ref[...] + cw_ref[LRU_CONV - 1:LRU_CONV, :] * xe_ref[halo:halo + ts, :]
    for j in range(LRU_CONV - 1):
        off = halo - (LRU_CONV - 1) + j
        xa = xa + cw_ref[j:j + 1, :] * xe_ref[off:off + ts, :]
    xe_ref[0:halo, :] = xe_ref[ts:ts + halo, :]
    xa_ref[...] = xa
    ri_ref[...] = jnp.dot(xa.astype(BF16), wri_ref[...], preferred_element_type=F32) + bri_ref[...]


def _lru_rows(r0, carry, softplus, xa_ref, ri_ref, zg_ref, o_ref):
    rows = slice(r0, r0 + LRU_ROWS)
    xa = xa_ref[rows, :]
    r = _sigmoid(ri_ref[rows, 0:D_LRU])
    gi = _sigmoid(ri_ref[rows, D_LRU:2 * D_LRU])
    a = jnp.exp((-LRU_C) * r * softplus)
    u = jnp.sqrt(1.0 - a * a) * (gi * xa)
    row = lax.broadcasted_iota(jnp.int32, (LRU_ROWS, D_LRU), 0) & (SUBLANES - 1)
    s = 1
    while s < SUBLANES:
        keep = row >= s
        u = jnp.where(keep, a * pltpu.roll(u, s, 0) + u, u)
        a = jnp.where(keep, a * pltpu.roll(a, s, 0), a)
        s *= 2
    hs = []
    for g0 in range(0, LRU_ROWS, SUBLANES):
        h = a[g0:g0 + SUBLANES, :] * carry + u[g0:g0 + SUBLANES, :]
        hs.append(h)
        carry = jnp.broadcast_to(h[SUBLANES - 1:SUBLANES, :], (SUBLANES, D_LRU))
    o_ref[rows, :] = (jnp.concatenate(hs, axis=0) * _gelu_tanh(zg_ref[rows, :])).astype(o_ref.dtype)
    return carry


CONV_FIRST_TAP = CONV_HALO - (CONV_WIDTH - 1)


def _cconv_shifts(ce_ref, sh_ref):
    for r in range(1, SUBLANES):
        sh_ref[r - 1] = ce_ref[r:r + sh_ref.shape[1], :]


def _cconv_rows(r0, b_ref, lg_ref, lb_ref, o_ref, ce_ref, sh_ref, wb_ref):
    groups = range(0, CONV_ROWS, SUBLANES)
    accs = [jnp.broadcast_to(b_ref[...], (SUBLANES, D_CONV)) for _ in groups]
    for j in range(CONV_WIDTH):
        r = (CONV_FIRST_TAP + j) % SUBLANES
        base = r0 + CONV_FIRST_TAP + j - r
        wb = wb_ref[j]
        for k, g0 in enumerate(groups):
            lo = base + g0
            src = ce_ref[lo:lo + SUBLANES, :] if r == 0 else sh_ref[r - 1, lo:lo + SUBLANES, :]
            accs[k] = accs[k] + wb * src
    acc = jnp.concatenate(accs, axis=0)
    mu = jnp.mean(acc, axis=-1, keepdims=True)
    xc = acc - mu
    y = xc * lax.rsqrt(jnp.mean(xc * xc, axis=-1, keepdims=True) + EPS) * lg_ref[...] + lb_ref[...]
    o_ref[r0:r0 + CONV_ROWS, :] = (y * _sigmoid(y)).astype(o_ref.dtype)


def _mix_kernel(x_ref, g_ref, w_ref, bg_ref,
                lcw_ref, lcb_ref, wri_ref, bri_ref, lam_ref,
                ccw_ref, ccb_ref, lng_ref, lnb_ref,
                z_ref, ya_ref, yb_ref,
                hb_ref, xe_ref, xa_ref, ri_ref, zg_ref, carry_ref, sp_ref, ce_ref, sh_ref, wb_ref):
    ts = x_ref.shape[0]
    first = pl.program_id(1) == 0
    def region(k):
        return pl.program_id(1) > -1 - k

    def proj(c0):
        return jnp.dot(hb_ref[...], w_ref[:, c0:c0 + MIX_COL_CHUNK], preferred_element_type=F32)

    def stored_chunk(c0):
        y = proj(IN_Q + c0)
        if c0 < N_GATE:
            y = _sigmoid(y + bg_ref[:, c0:c0 + MIX_COL_CHUNK])
        elif c0 < COL_K:
            y = y * DA_Q_SCALE
        z_ref[:, c0:c0 + MIX_COL_CHUNK] = y.astype(z_ref.dtype)

    def conv_blocks(lo, hi):
        for k in range(lo, hi):
            _cconv_rows(k * CONV_ROWS, ccb_ref, lng_ref, lnb_ref, yb_ref, ce_ref, sh_ref, wb_ref)

    def lru_blocks(lo, hi):
        carry = carry_ref[...]
        for k in range(lo, hi):
            carry = _lru_rows(k * LRU_ROWS, carry, sp_ref[...], xa_ref, ri_ref, zg_ref, ya_ref)
        carry_ref[...] = carry

    _branch_init(first, ccw_ref, lam_ref, xe_ref, carry_ref, ce_ref, wb_ref, sp_ref)
    hb_ref[...] = _rms_norm(x_ref[...], g_ref[...]).astype(BF16)
    ce_ref[CONV_HALO:CONV_HALO + ts, :] = proj(IN_CONV_A) * _sigmoid(proj(IN_CONV_G))

    n_conv, n_lru = ts // CONV_ROWS, ts // LRU_ROWS
    early_conv = max(n_conv // 8, 1)

    @pl.when(region(0))
    def _():
        xe_ref[SUBLANES:SUBLANES + ts, :] = proj(IN_LRU_X)
        zg_ref[...] = proj(IN_LRU_G)
        _cconv_shifts(ce_ref, sh_ref)

    @pl.when(region(1))
    def _():
        _lru_gates(lcw_ref, lcb_ref, wri_ref, bri_ref, xe_ref, xa_ref, ri_ref)
        conv_blocks(0, early_conv)

    stored = list(range(0, N_MIX, MIX_COL_CHUNK))
    for i, c0 in enumerate(stored):
        conv_lo = early_conv + (i * (n_conv - early_conv)) // len(stored)
        conv_hi = early_conv + ((i + 1) * (n_conv - early_conv)) // len(stored)
        lru_lo, lru_hi = (i * n_lru) // len(stored), ((i + 1) * n_lru) // len(stored)

        @pl.when(region(2 + i))
        def _(c0=c0, conv_lo=conv_lo, conv_hi=conv_hi, lru_lo=lru_lo, lru_hi=lru_hi):
            conv_blocks(conv_lo, conv_hi)
            lru_blocks(lru_lo, lru_hi)
            stored_chunk(c0)

    ce_ref[0:CONV_HALO, :] = ce_ref[ts:ts + CONV_HALO, :]


def _mix_stage(x2d, batch, seq, g, w, bg, lcw, lcb, wri, bri, lam, ccw, ccb, lng, lnb):
    nt = seq // SEQ_TILE
    assert D_LRU == MIX_COL_CHUNK and D_CONV == MIX_COL_CHUNK and COL_K - COL_Q == MIX_COL_CHUNK
    assert N_GATE % MIX_COL_CHUNK == 0 and N_MIX % MIX_COL_CHUNK == 0
    tile = lambda width: pl.BlockSpec((SEQ_TILE, width), lambda b, t: (b * nt + t, 0))
    n = batch * seq
    return pl.pallas_call(
        _mix_kernel,
        grid=(batch, nt),
        in_specs=[
            tile(D_MODEL),
            _resident((1, D_MODEL)),
            _resident((D_MODEL, IN_Q + N_MIX)),
            _resident((1, N_GATE)),
            _resident((LRU_CONV, D_LRU)),
            _resident((1, D_LRU)),
            _resident((D_LRU, 2 * D_LRU)),
            _resident((1, 2 * D_LRU)),
            _resident((1, D_LRU)),
            _resident((CONV_WIDTH, D_CONV)),
            _resident((1, D_CONV)),
            _resident((1, D_CONV)),
            _resident((1, D_CONV)),
        ],
        out_specs=[tile(N_MIX), tile(D_LRU), tile(D_CONV)],
        out_shape=[
            jax.ShapeDtypeStruct((n, N_MIX), BF16),
            jax.ShapeDtypeStruct((n, D_LRU), BF16),
            jax.ShapeDtypeStruct((n, D_CONV), BF16),
        ],
        scratch_shapes=[
            pltpu.VMEM((SEQ_TILE, D_MODEL), BF16),
            pltpu.VMEM((SEQ_TILE + SUBLANES, D_LRU), F32),
            pltpu.VMEM((SEQ_TILE, D_LRU), F32),
            pltpu.VMEM((SEQ_TILE, 2 * D_LRU), F32),
            pltpu.VMEM((SEQ_TILE, D_LRU), F32),
            pltpu.VMEM((SUBLANES, D_LRU), F32),
            pltpu.VMEM((1, D_LRU), F32),
            pltpu.VMEM((SEQ_TILE + CONV_HALO, D_CONV), F32),
            pltpu.VMEM((SUBLANES - 1, SEQ_TILE + CONV_HALO - SUBLANES, D_CONV), F32),
            pltpu.VMEM((CONV_WIDTH, SUBLANES, D_CONV), F32),
        ],
        compiler_params=_params(("parallel", "arbitrary")),
        name="mix_stage",
    )(x2d, g, w, bg, lcw, lcb, wri, bri, lam, ccw, ccb, lng, lnb)


def _t5_bucket(rel):
    nb = REL_BUCKETS // 2
    ret = jnp.where(rel > 0, nb, 0)
    n = jnp.abs(rel)
    max_exact = nb // 2
    large = max_exact + (jnp.log(jnp.maximum(n, 1).astype(F32) / max_exact)
                         / math.log(REL_MAX_DIST / max_exact) * (nb - max_exact)).astype(jnp.int32)
    large = jnp.minimum(large, nb - 1)
    return ret + jnp.where(n < max_exact, n, large)


def _bias_tiles_kernel(tbl_ref, bucket_ref, o_ref, *, far_bucket):
    h = pl.program_id(0)
    bucket = bucket_ref[...]
    acc = jnp.zeros(bucket.shape, F32)
    for bkt in range(REL_BUCKETS):
        acc = jnp.where(bucket == bkt, tbl_ref[bkt, h], acc)
    acc = (acc - tbl_ref[far_bucket, h]) * LOG2E
    o_ref[...] = jnp.where(bucket < 0, MASK_VALUE, acc)


def _bias_tiles(rel_bias):
    kpos = jnp.arange(DA_TK, dtype=jnp.int32)[:, None]
    qpos = jnp.arange(DA_TQ, dtype=jnp.int32)[None, :]
    prev = _t5_bucket(kpos - DA_TK - qpos)
    diag = jnp.where(kpos // CHUNK <= qpos // CHUNK, _t5_bucket(kpos - qpos), -1)
    buckets = jnp.stack([prev, diag])
    assert DA_TK >= REL_MAX_DIST and DA_TQ == DA_TK
    far_bucket = REL_BUCKETS // 2 - 1
    return pl.pallas_call(
        functools.partial(_bias_tiles_kernel, far_bucket=far_bucket),
        grid=(DA_HEADS,),
        in_specs=[
            pl.BlockSpec(memory_space=pltpu.SMEM),
            pl.BlockSpec((2, DA_TK, DA_TQ), lambda h: (0, 0, 0)),
        ],
        out_specs=pl.BlockSpec((None, 2, DA_TK, DA_TQ), lambda h: (h, 0, 0, 0)),
        out_shape=jax.ShapeDtypeStruct((DA_HEADS, 2, DA_TK, DA_TQ), F32),
        compiler_params=_params(("parallel",)),
        name="bias_tiles",
    )(rel_bias, buckets)


def _da_kernel(q_ref, k_ref, v_ref, bias_ref, lam_ref, sg_ref, o_ref,
               vt_ref, m_ref, l_ref, acc_ref, s_ref, *, lam_init):
    i = pl.program_id(1)
    tq, tk = DA_TQ, DA_TK
    blk = 2 * DA_DIM
    chains = [(h, c) for h in range(DA_HEADS) for c in range(2)]

    @pl.when(i == 0)
    def _():
        ones = jnp.ones((DA_ONES_ROWS, 2 * tk), BF16)
        for h in range(DA_HEADS):
            for jp in range(vt_ref.shape[1]):
                for half in range(2):
                    r0 = (2 * jp + half) * tk
                    v = v_ref[r0:r0 + tk, h * DA_VDIM:(h + 1) * DA_VDIM]
                    vt_ref[h, jp, 0:DA_VDIM, half * tk:(half + 1) * tk] = v.astype(F32).T.astype(BF16)
                vt_ref[h, jp, DA_VDIM:DA_VDIM + DA_ONES_ROWS, :] = ones

    lane = lax.broadcasted_iota(jnp.int32, (tq, blk), 1)
    qz = []
    for h, c in chains:
        q = q_ref[:, h * blk:(h + 1) * blk]
        qz.append(jnp.where((lane >= DA_DIM) == (c == 1), q, jnp.zeros_like(q)))

    m_ref[...] = jnp.full(m_ref.shape, MASK_VALUE, F32)
    l_ref[...] = jnp.zeros(l_ref.shape, F32)
    acc_ref[...] = jnp.zeros(acc_ref.shape, F32)

    def step(jp, nears):
        nkeys = len(nears) * tk
        k0 = pl.multiple_of(jp * (2 * tk), 2 * tk)

        def logits(n):
            h = chains[n][0]
            kb = k_ref[pl.ds(k0, nkeys), h * blk:(h + 1) * blk]
            s = lax.dot_general(kb, qz[n], (((1,), (1,)), ((), ())), preferred_element_type=F32)
            if all(near is None for near in nears):
                return s
            parts = [s[t * tk:(t + 1) * tk, :] if near is None else s[t * tk:(t + 1) * tk, :] + bias_ref[h, near]
                     for t, near in enumerate(nears)]
            return jnp.concatenate(parts, axis=0)

        block_max = []
        for n in range(len(chains)):
            s = logits(n)
            s_ref[n, 0:nkeys, :] = s
            block_max.append(jnp.max(s, axis=0, keepdims=True))
        alphas, probs = [], []
        for n in range(len(chains)):
            m_prev = m_ref[n]
            m_new = jnp.maximum(m_prev, block_max[n])
            alphas.append(jnp.exp2(m_prev - m_new))
            probs.append(jnp.exp2(s_ref[n, 0:nkeys, :] - m_new).astype(BF16))
            m_ref[n] = m_new
        for n, (h, c) in enumerate(chains):
            pv = jnp.dot(vt_ref[h, jp, :, 0:nkeys], probs[n], preferred_element_type=F32)
            acc_ref[n] = alphas[n] * acc_ref[n] + pv[0:DA_VDIM, :]
            l_ref[n] = alphas[n] * l_ref[n] + pv[DA_VDIM:DA_VDIM + 1, :]

    def far_body(jp, carry):
        step(jp, (None, None))
        return carry

    lax.fori_loop(0, jnp.maximum(i - 1, 0) // 2, far_body, 0)
    odd = (i % 2) == 1

    @pl.when(odd)
    def _():
        step((i - 1) // 2, (0, 1))

    @pl.when(jnp.logical_and(jnp.logical_not(odd), i >= 2))
    def _():
        step((i - 2) // 2, (None, 0))

    @pl.when(jnp.logical_not(odd))
    def _():
        step(i // 2, (1,))

    lv = lam_ref[...]
    lam = (jnp.exp(jnp.sum(lv[0:1, :] * lv[1:2, :], axis=-1, keepdims=True))
           - jnp.exp(jnp.sum(lv[2:3, :] * lv[3:4, :], axis=-1, keepdims=True)) + lam_init)
    for h in range(DA_HEADS):
        d = acc_ref[2 * h] / l_ref[2 * h] - lam * (acc_ref[2 * h + 1] / l_ref[2 * h + 1])
        y = d * lax.rsqrt(jnp.mean(d * d, axis=0, keepdims=True) + EPS)
        o_ref[:, h * DA_VDIM:(h + 1) * DA_VDIM] = (y.T * sg_ref[...] * (1.0 - lam_init)).astype(o_ref.dtype)


def _diff_attention(zmix, batch, seq, bias_tiles, lam_vec, subln_g, lam_init):
    nq = seq // DA_TQ
    assert COL_Q % D_DA == 0 and COL_K % D_DA == 0 and COL_V % D_DA == 0
    return pl.pallas_call(
        functools.partial(_da_kernel, lam_init=lam_init),
        grid=(batch, nq),
        in_specs=[
            pl.BlockSpec((DA_TQ, D_DA), lambda b, i: (b * nq + i, COL_Q // D_DA)),
            pl.BlockSpec((seq, D_DA), lambda b, i: (b, COL_K // D_DA)),
            pl.BlockSpec((seq, D_DA), lambda b, i: (b, COL_V // D_DA)),
            _resident((DA_HEADS, 2, DA_TK, DA_TQ)),
            _resident((4, DA_DIM)),
            _resident((1, DA_VDIM)),
        ],
        out_specs=pl.BlockSpec((DA_TQ, D_DA), lambda b, i: (b * nq + i, 0)),
        out_shape=jax.ShapeDtypeStruct((batch * seq, D_DA), BF16),
        scratch_shapes=[
            pltpu.VMEM((DA_HEADS, seq // (2 * DA_TK), DA_VDIM + DA_ONES_ROWS, 2 * DA_TK), BF16),
            pltpu.VMEM((2 * DA_HEADS, 1, DA_TQ), F32),
            pltpu.VMEM((2 * DA_HEADS, 1, DA_TQ), F32),
            pltpu.VMEM((2 * DA_HEADS, DA_VDIM, DA_TQ), F32),
            pltpu.VMEM((2 * DA_HEADS, 2 * DA_TK, DA_TQ), F32),
        ],
        compiler_params=_params(("parallel", "arbitrary")),
        name="diff_attention",
    )(zmix, zmix, zmix, bias_tiles, lam_vec, subln_g)


def _merge_kernel(x_ref, ya_ref, yb_ref, yc_ref, g0_ref, g1_ref, g2_ref,
                  wa_ref, wb_ref, wc_ref, wo_ref, o_ref):
    merged = (g0_ref[...].astype(F32) * jnp.dot(ya_ref[...], wa_ref[...], preferred_element_type=F32)
              + g1_ref[...].astype(F32) * jnp.dot(yb_ref[...], wb_ref[...], preferred_element_type=F32)
              + g2_ref[...].astype(F32) * jnp.dot(yc_ref[...], wc_ref[...], preferred_element_type=F32))
    o_ref[...] = x_ref[...] + jnp.dot(merged.astype(BF16), wo_ref[...], preferred_element_type=F32)


def _merge(x2d, ya, yb, yc, zmix, wa, wb, wc, wo):
    n = x2d.shape[0]
    row = lambda i: (i, 0)
    return pl.pallas_call(
        _merge_kernel,
        grid=(n // ROW_TILE,),
        in_specs=[
            pl.BlockSpec((ROW_TILE, D_MODEL), row),
            pl.BlockSpec((ROW_TILE, D_LRU), row),
            pl.BlockSpec((ROW_TILE, D_CONV), row),
            pl.BlockSpec((ROW_TILE, D_DA), row),
            pl.BlockSpec((ROW_TILE, D_MODEL), lambda i: (i, 0)),
            pl.BlockSpec((ROW_TILE, D_MODEL), lambda i: (i, 1)),
            pl.BlockSpec((ROW_TILE, D_MODEL), lambda i: (i, 2)),
            _resident((D_LRU, D_MODEL)),
            _resident((D_CONV, D_MODEL)),
            _resident((D_DA, D_MODEL)),
            _resident((D_MODEL, D_MODEL)),
        ],
        out_specs=pl.BlockSpec((ROW_TILE, D_MODEL), row),
        out_shape=jax.ShapeDtypeStruct((n, D_MODEL), F32),
        compiler_params=_params(("parallel",)),
        name="merge",
    )(x2d, ya, yb, yc, zmix, zmix, zmix, wa, wb, wc, wo)


def _xattn_kernel(x_ref, g_ref, wq_ref, kv_ref, wo_ref, o_ref, oc_ref):
    x = x_ref[...]
    hq = _rms_norm(x, g_ref[...]).astype(BF16)
    q = (jnp.dot(hq, wq_ref[...], preferred_element_type=F32) * (XA_DIM ** -0.5)).astype(BF16)
    for h in range(XA_HEADS):
        c0 = h * XA_DIM
        kh = kv_ref[:, c0:c0 + XA_DIM]
        vh = kv_ref[:, D_MODEL + c0:D_MODEL + c0 + XA_DIM]
        s = lax.dot_general(q[:, c0:c0 + XA_DIM], kh, (((1,), (1,)), ((), ())), preferred_element_type=F32)
        p = jnp.exp(s - jnp.max(s, axis=-1, keepdims=True))
        l = jnp.sum(p, axis=-1, keepdims=True)
        o = jnp.dot(p.astype(BF16), vh, preferred_element_type=F32) / l
        oc_ref[:, c0:c0 + XA_DIM] = o.astype(BF16)
    o_ref[...] = x + jnp.dot(oc_ref[...], wo_ref[...], preferred_element_type=F32)


def _cross_attention(x2d, batch, seq, g, wq, kv, wo):
    nt = seq // ROW_TILE
    n_mem = kv.shape[0] // batch
    return pl.pallas_call(
        _xattn_kernel,
        grid=(batch, nt),
        in_specs=[
            pl.BlockSpec((ROW_TILE, D_MODEL), lambda b, t: (b * nt + t, 0)),
            _resident((1, D_MODEL)),
            _resident((D_MODEL, D_MODEL)),
            pl.BlockSpec((n_mem, 2 * D_MODEL), lambda b, t: (b, 0)),
            _resident((D_MODEL, D_MODEL)),
        ],
        out_specs=pl.BlockSpec((ROW_TILE, D_MODEL), lambda b, t: (b * nt + t, 0)),
        out_shape=jax.ShapeDtypeStruct(x2d.shape, F32),
        scratch_shapes=[pltpu.VMEM((ROW_TILE, D_MODEL), BF16)],
        compiler_params=_params(("parallel", "arbitrary")),
        name="cross_attention",
    )(x2d, g, wq, kv, wo)


def _ffn_kernel(x_ref, g_ref, w1_ref, w3_ref, cw_ref, cb_ref, w2_ref, fg_ref, o_ref,
                ae_ref, halo_ref, hm_ref, *, final_norm):
    ts = x_ref.shape[0]
    halo = SUBLANES

    @pl.when(pl.program_id(1) == 0)
    def _():
        halo_ref[...] = jnp.zeros(halo_ref.shape, F32)

    x = x_ref[...]
    hf = _rms_norm(x, g_ref[...]).astype(BF16)
    for c0 in range(0, D_FF, FFN_COL_CHUNK):
        cs = slice(c0, c0 + FFN_COL_CHUNK)
        a = jnp.dot(hf, w1_ref[:, cs], preferred_element_type=F32)
        ae_ref[0:halo, :] = halo_ref[:, cs]
        ae_ref[halo:halo + ts, :] = a
        halo_ref[:, cs] = a[ts - halo:ts, :]
        y = cb_ref[:, cs] + cw_ref[FFN_CONV - 1:FFN_CONV, cs] * a
        for j in range(FFN_CONV - 1):
            off = halo - (FFN_CONV - 1) + j
            y = y + cw_ref[j:j + 1, cs] * ae_ref[off:off + ts, :]
        up = jnp.dot(hf, w3_ref[:, cs], preferred_element_type=F32)
        hm_ref[:, cs] = (y * _sigmoid(y) * up).astype(BF16)
    out = x + jnp.dot(hm_ref[...], w2_ref[...], preferred_element_type=F32)
    if final_norm:
        out = _rms_norm(out, fg_ref[...])
    o_ref[...] = out


def _ffn(x2d, batch, seq, g, w1, w3, cw, cb, w2, fg, final_norm):
    nt = seq // SEQ_TILE
    return pl.pallas_call(
        functools.partial(_ffn_kernel, final_norm=final_norm),
        grid=(batch, nt),
        in_specs=[
            pl.BlockSpec((SEQ_TILE, D_MODEL), lambda b, t: (b * nt + t, 0)),
            _resident((1, D_MODEL)),
            _resident((D_MODEL, D_FF)),
            _resident((D_MODEL, D_FF)),
            _resident((FFN_CONV, D_FF)),
            _resident((1, D_FF)),
            _resident((D_FF, D_MODEL)),
            _resident((1, D_MODEL)),
        ],
        out_specs=pl.BlockSpec((SEQ_TILE, D_MODEL), lambda b, t: (b * nt + t, 0)),
        out_shape=jax.ShapeDtypeStruct(x2d.shape, F32),
        scratch_shapes=[
            pltpu.VMEM((SEQ_TILE + SUBLANES, FFN_COL_CHUNK), F32),
            pltpu.VMEM((SUBLANES, D_FF), F32),
            pltpu.VMEM((SEQ_TILE, D_FF), BF16),
        ],
        compiler_params=_params(("parallel", "arbitrary")),
        name="ffn_final" if final_norm else "ffn",
    )(x2d, g, w1, w3, cw, cb, w2, fg)


def _block_diag(w):
    nb, bi, bj = w.shape
    eye = jnp.eye(nb, dtype=w.dtype)
    return jnp.einsum('hij,hg->higj', w, eye).reshape(nb * bi, nb * bj)


def kernel(x, mem, rel_bias, norm_mix_g, w_in, w_gate, b_gate, lru_conv_w, lru_conv_b, lru_wr, lru_br, lru_wi, lru_bi, lru_lambda, lru_out, cm_conv_w, cm_conv_b, cm_ln_g, cm_ln_b, cm_out, da_lambda, da_subln_g, da_out, w_o, norm_xa_g, norm_mem_g, xa_wq, xa_wkv, xa_wo, norm_ffn_g, ffn_w1, ffn_w3, ffn_conv_w, ffn_conv_b, ffn_w2, final_g):
    batch, seq, d = x.shape
    n_mem = mem.shape[1]
    depth = w_in.shape[0]
    assert d == D_MODEL and seq % SEQ_TILE == 0 and seq % ROW_TILE == 0 and seq % DA_TQ == 0

    bias_tiles = _bias_tiles(rel_bias)
    x2d = x.reshape(batch * seq, d)
    mem2d = mem.reshape(batch * n_mem, d)
    row = lambda v: v.reshape(1, -1)
    assert w_in.shape[2] == D_IN

    for l in range(depth):
        w_mix = jnp.concatenate([w_in[l, :, :IN_Q], w_gate[l, 0], w_gate[l, 1], w_gate[l, 2], w_in[l, :, IN_Q:]],
                                axis=1).astype(BF16)
        wri = jnp.concatenate([_block_diag(lru_wr[l]), _block_diag(lru_wi[l])], axis=1).astype(BF16)
        bri = jnp.concatenate([lru_br[l], lru_bi[l]]).reshape(1, -1)
        zmix, ya, yb = _mix_stage(
            x2d, batch, seq, row(norm_mix_g[l]), w_mix, row(b_gate[l]),
            lru_conv_w[l], row(lru_conv_b[l]), wri, bri, row(lru_lambda[l]),
            cm_conv_w[l], row(cm_conv_b[l]), row(cm_ln_g[l]), row(cm_ln_b[l]))
        lam_init = 0.8 - 0.6 * math.exp(-0.3 * l)
        yc = _diff_attention(zmix, batch, seq, bias_tiles, da_lambda[l], row(da_subln_g[l]), lam_init)
        x2d = _merge(x2d, ya, yb, yc, zmix, lru_out[l].astype(BF16), cm_out[l].astype(BF16),
                     da_out[l].astype(BF16), w_o[l].astype(BF16))
        kv = _norm_proj(mem2d, row(norm_mem_g[l]), xa_wkv[l].astype(BF16), "mem_kv")
        x2d = _cross_attention(x2d, batch, seq, row(norm_xa_g[l]), xa_wq[l].astype(BF16), kv,
                               xa_wo[l].astype(BF16))
        x2d = _ffn(x2d, batch, seq, row(norm_ffn_g[l]), ffn_w1[l].astype(BF16), ffn_w3[l].astype(BF16),
                   ffn_conv_w[l], row(ffn_conv_b[l]), ffn_w2[l].astype(BF16), row(final_g),
                   final_norm=(l == depth - 1))
    return x2d.reshape(batch, seq, d)
```

```python
import functools
import math

import jax
import jax.numpy as jnp
from jax import lax
from jax.experimental import pallas as pl
from jax.experimental.pallas import tpu as pltpu

F32 = jnp.float32
BF16 = jnp.bfloat16

D_MODEL = 1024
CHUNK = 64
D_LRU = 512
LRU_BLOCKS = 8
LRU_CONV = 4
LRU_C = 8.0
D_CONV = 512
CONV_WIDTH = 31
DA_HEADS = 4
DA_DIM = 64
DA_VDIM = 2 * DA_DIM
D_DA = DA_HEADS * DA_VDIM
REL_BUCKETS = 32
REL_MAX_DIST = 128
XA_HEADS = 4
XA_DIM = D_MODEL // XA_HEADS
D_FF = 2816
FFN_CONV = 3
N_BRANCH = 3
EPS = 1e-6

IN_LRU_X = 0
IN_LRU_G = IN_LRU_X + D_LRU
IN_CONV_A = IN_LRU_G + D_LRU
IN_CONV_G = IN_CONV_A + D_CONV
IN_Q = IN_CONV_G + D_CONV
IN_K = IN_Q + DA_HEADS * 2 * DA_DIM
IN_V = IN_K + DA_HEADS * 2 * DA_DIM
D_IN = IN_V + D_DA
N_GATE = N_BRANCH * D_MODEL
COL_Q = N_GATE
COL_K = COL_Q + DA_HEADS * 2 * DA_DIM
COL_V = COL_K + DA_HEADS * 2 * DA_DIM
COL_LRU_X = COL_V + D_DA
COL_LRU_G = COL_LRU_X + D_LRU
COL_CONV_A = COL_LRU_G + D_LRU
COL_CONV_G = COL_CONV_A + D_CONV
N_MIX = COL_CONV_G + D_CONV

SUBLANES = 8
LANES = 128
VMEM_LIMIT = 56 * 1024 * 1024
ROW_TILE = 512
MIX_COL_CHUNK = 512
SEQ_TILE = 512
CONV_ROWS = 32
LRU_ROWS = 32
CONV_HALO = 32
CONV_FIRST_TAP = CONV_HALO - (CONV_WIDTH - 1)
DA_TQ = 256
DA_TK = 256
FFN_COL_CHUNK = 256
DA_ONES_ROWS = 16
MASK_VALUE = -1e30
LOG2E = math.log2(math.e)
DA_Q_SCALE = DA_DIM ** -0.5 * LOG2E


def _resident(shape):
    nd = len(shape)
    return pl.BlockSpec(shape, lambda *_: (0,) * nd, pipeline_mode=pl.Buffered(1))


def _params(semantics):
    return pltpu.CompilerParams(dimension_semantics=semantics, vmem_limit_bytes=VMEM_LIMIT)


def _rms_norm(x, g):
    return x * lax.rsqrt(jnp.mean(x * x, axis=-1, keepdims=True) + EPS) * g


def _sigmoid(x):
    return 1.0 / (1.0 + jnp.exp(-x))


def _gelu_tanh(x):
    c = math.sqrt(2.0 / math.pi)
    return 0.5 * x * (1.0 + jnp.tanh(c * (x + 0.044715 * (x * x * x))))


def _norm_proj_kernel(x_ref, g_ref, w_ref, b_ref, o_ref, *, n_sig, scaled):
    h = _rms_norm(x_ref[...], g_ref[...]).astype(BF16)
    for c0 in range(0, o_ref.shape[1], MIX_COL_CHUNK):
        y = jnp.dot(h, w_ref[:, c0:c0 + MIX_COL_CHUNK], preferred_element_type=F32)
        if c0 < n_sig:
            y = _sigmoid(y + b_ref[:, c0:c0 + MIX_COL_CHUNK])
        elif scaled is not None and scaled[0] <= c0 < scaled[1]:
            y = y * scaled[2]
        o_ref[:, c0:c0 + MIX_COL_CHUNK] = y.astype(o_ref.dtype)


def _norm_proj(x2d, g, w, b, name, n_sig=0, scaled=None):
    n, d = x2d.shape
    n_out = w.shape[1]
    assert n % ROW_TILE == 0 and n_out % MIX_COL_CHUNK == 0 and n_sig % MIX_COL_CHUNK == 0
    assert scaled is None or (scaled[0] % MIX_COL_CHUNK == 0 and scaled[1] % MIX_COL_CHUNK == 0)
    return pl.pallas_call(
        functools.partial(_norm_proj_kernel, n_sig=n_sig, scaled=scaled),
        grid=(n // ROW_TILE,),
        in_specs=[
            pl.BlockSpec((ROW_TILE, d), lambda i: (i, 0)),
            _resident((1, d)),
            _resident((d, n_out)),
            _resident((1, b.shape[1])),
        ],
        out_specs=pl.BlockSpec((ROW_TILE, n_out), lambda i: (i, 0)),
        out_shape=jax.ShapeDtypeStruct((n, n_out), BF16),
        compiler_params=_params(("parallel",)),
        name=name,
    )(x2d, g, w, b)


def _lru_rows(r0, carry, softplus, xa_ref, ri_ref, zg_ref, o_ref):
    rows = slice(r0, r0 + LRU_ROWS)
    xa = xa_ref[rows, :]
    r = _sigmoid(ri_ref[rows, 0:D_LRU])
    gi = _sigmoid(ri_ref[rows, D_LRU:2 * D_LRU])
    a = jnp.exp((-LRU_C) * r * softplus)
    u = jnp.sqrt(1.0 - a * a) * (gi * xa)
    row = lax.broadcasted_iota(jnp.int32, (LRU_ROWS, D_LRU), 0) & (SUBLANES - 1)
    s = 1
    while s < SUBLANES:
        keep = row >= s
        u = jnp.where(keep, a * pltpu.roll(u, s, 0) + u, u)
        a = jnp.where(keep, a * pltpu.roll(a, s, 0), a)
        s *= 2
    hs = []
    for g0 in range(0, LRU_ROWS, SUBLANES):
        h = a[g0:g0 + SUBLANES, :] * carry + u[g0:g0 + SUBLANES, :]
        hs.append(h)
        carry = jnp.broadcast_to(h[SUBLANES - 1:SUBLANES, :], (SUBLANES, D_LRU))
    gate = _gelu_tanh(zg_ref[rows, :].astype(F32))
    o_ref[rows, :] = (jnp.concatenate(hs, axis=0) * gate).astype(o_ref.dtype)
    return carry


def _lru_kernel(zx_ref, zg_ref, cw_ref, cb_ref, wri_ref, bri_ref, lam_ref, o_ref,
                xe_ref, xa_ref, ri_ref, carry_ref, sp_ref):
    ts = zx_ref.shape[0]
    halo = SUBLANES

    @pl.when(pl.program_id(1) == 0)
    def _():
        xe_ref[0:halo, :] = jnp.zeros((halo, D_LRU), F32)
        carry_ref[...] = jnp.zeros((SUBLANES, D_LRU), F32)
        nlam = -lam_ref[...]
        sp_ref[...] = jnp.maximum(nlam, 0.0) + jnp.log(1.0 + jnp.exp(-jnp.abs(nlam)))

    xe_ref[halo:halo + ts, :] = zx_ref[...].astype(F32)
    xa = cb_ref[...] + cw_ref[LRU_CONV - 1:LRU_CONV, :] * xe_ref[halo:halo + ts, :]
    for j in range(LRU_CONV - 1):
        off = halo - (LRU_CONV - 1) + j
        xa = xa + cw_ref[j:j + 1, :] * xe_ref[off:off + ts, :]
    xe_ref[0:halo, :] = xe_ref[ts:ts + halo, :]
    xa_ref[...] = xa
    ri_ref[...] = jnp.dot(xa.astype(BF16), wri_ref[...], preferred_element_type=F32) + bri_ref[...]

    carry = carry_ref[...]
    for r0 in range(0, ts, LRU_ROWS):
        carry = _lru_rows(r0, carry, sp_ref[...], xa_ref, ri_ref, zg_ref, o_ref)
    carry_ref[...] = carry


def _lru_branch(zmix, batch, seq, cw, cb, wri, bri, lam):
    nt = seq // SEQ_TILE
    return pl.pallas_call(
        _lru_kernel,
        grid=(batch, nt),
        in_specs=[
            pl.BlockSpec((SEQ_TILE, D_LRU), lambda b, t: (b * nt + t, COL_LRU_X // D_LRU)),
            pl.BlockSpec((SEQ_TILE, D_LRU), lambda b, t: (b * nt + t, COL_LRU_G // D_LRU)),
            _resident((LRU_CONV, D_LRU)),
            _resident((1, D_LRU)),
            _resident((D_LRU, 2 * D_LRU)),
            _resident((1, 2 * D_LRU)),
            _resident((1, D_LRU)),
        ],
        out_specs=pl.BlockSpec((SEQ_TILE, D_LRU), lambda b, t: (b * nt + t, 0)),
        out_shape=jax.ShapeDtypeStruct((batch * seq, D_LRU), BF16),
        scratch_shapes=[
            pltpu.VMEM((SEQ_TILE + SUBLANES, D_LRU), F32),
            pltpu.VMEM((SEQ_TILE, D_LRU), F32),
            pltpu.VMEM((SEQ_TILE, 2 * D_LRU), F32),
            pltpu.VMEM((SUBLANES, D_LRU), F32),
            pltpu.VMEM((1, D_LRU), F32),
        ],
        compiler_params=_params(("parallel", "arbitrary")),
        name="lru_branch",
    )(zmix, zmix, cw, cb, wri, bri, lam)


def _cconv_rows(r0, b_ref, lg_ref, lb_ref, o_ref, ce_ref, sh_ref, wb_ref):
    groups = range(0, CONV_ROWS, SUBLANES)
    accs = [jnp.broadcast_to(b_ref[...], (SUBLANES, D_CONV)) for _ in groups]
    for j in range(CONV_WIDTH):
        r = (CONV_FIRST_TAP + j) % SUBLANES
        base = r0 + CONV_FIRST_TAP + j - r
        wb = wb_ref[j]
        for k, g0 in enumerate(groups):
            lo = base + g0
            src = ce_ref[lo:lo + SUBLANES, :] if r == 0 else sh_ref[r - 1, lo:lo + SUBLANES, :]
            accs[k] = accs[k] + wb * src
    acc = jnp.concatenate(accs, axis=0)
    mu = jnp.mean(acc, axis=-1, keepdims=True)
    xc = acc - mu
    y = xc * lax.rsqrt(jnp.mean(xc * xc, axis=-1, keepdims=True) + EPS) * lg_ref[...] + lb_ref[...]
    o_ref[r0:r0 + CONV_ROWS, :] = (y * _sigmoid(y)).astype(o_ref.dtype)


def _cconv_kernel(ca_ref, cg_ref, w_ref, b_ref, lg_ref, lb_ref, o_ref, ce_ref, sh_ref, wb_ref):
    ts = ca_ref.shape[0]

    @pl.when(pl.program_id(1) == 0)
    def _():
        ce_ref[0:CONV_HALO, :] = jnp.zeros((CONV_HALO, D_CONV), F32)
        for j in range(CONV_WIDTH):
            wb_ref[j] = jnp.broadcast_to(w_ref[j:j + 1, :], (SUBLANES, D_CONV))

    ce_ref[CONV_HALO:CONV_HALO + ts, :] = ca_ref[...].astype(F32) * _sigmoid(cg_ref[...].astype(F32))
    for r in range(1, SUBLANES):
        sh_ref[r - 1] = ce_ref[r:r + sh_ref.shape[1], :]
    for r0 in range(0, ts, CONV_ROWS):
        _cconv_rows(r0, b_ref, lg_ref, lb_ref, o_ref, ce_ref, sh_ref, wb_ref)
    ce_ref[0:CONV_HALO, :] = ce_ref[ts:ts + CONV_HALO, :]


def _cconv_branch(zmix, batch, seq, w, b, lg, lb):
    nt = seq // SEQ_TILE
    return pl.pallas_call(
        _cconv_kernel,
        grid=(batch, nt),
        in_specs=[
            pl.BlockSpec((SEQ_TILE, D_CONV), lambda b_, t: (b_ * nt + t, COL_CONV_A // D_CONV)),
            pl.BlockSpec((SEQ_TILE, D_CONV), lambda b_, t: (b_ * nt + t, COL_CONV_G // D_CONV)),
            _resident((CONV_WIDTH, D_CONV)),
            _resident((1, D_CONV)),
            _resident((1, D_CONV)),
            _resident((1, D_CONV)),
        ],
        out_specs=pl.BlockSpec((SEQ_TILE, D_CONV), lambda b_, t: (b_ * nt + t, 0)),
        out_shape=jax.ShapeDtypeStruct((batch * seq, D_CONV), BF16),
        scratch_shapes=[
            pltpu.VMEM((SEQ_TILE + CONV_HALO, D_CONV), F32),
            pltpu.VMEM((SUBLANES - 1, SEQ_TILE + CONV_HALO - SUBLANES, D_CONV), F32),
            pltpu.VMEM((CONV_WIDTH, SUBLANES, D_CONV), F32),
        ],
        compiler_params=_params(("parallel", "arbitrary")),
        name="cconv_branch",
    )(zmix, zmix, w, b, lg, lb)


def _t5_bucket(rel):
    nb = REL_BUCKETS // 2
    ret = jnp.where(rel > 0, nb, 0)
    n = jnp.abs(rel)
    max_exact = nb // 2
    large = max_exact + (jnp.log(jnp.maximum(n, 1).astype(F32) / max_exact)
                         / math.log(REL_MAX_DIST / max_exact) * (nb - max_exact)).astype(jnp.int32)
    large = jnp.minimum(large, nb - 1)
    return ret + jnp.where(n < max_exact, n, large)


def _bias_tiles_kernel(tbl_ref, bucket_ref, o_ref, *, far_bucket):
    h = pl.program_id(0)
    bucket = bucket_ref[...]
    acc = jnp.zeros(bucket.shape, F32)
    for bkt in range(REL_BUCKETS):
        acc = jnp.where(bucket == bkt, tbl_ref[bkt, h], acc)
    acc = (acc - tbl_ref[far_bucket, h]) * LOG2E
    o_ref[...] = jnp.where(bucket < 0, MASK_VALUE, acc)


def _bias_tiles(rel_bias):
    kpos = jnp.arange(DA_TK, dtype=jnp.int32)[:, None]
    qpos = jnp.arange(DA_TQ, dtype=jnp.int32)[None, :]
    prev = _t5_bucket(kpos - DA_TK - qpos)
    diag = jnp.where(kpos // CHUNK <= qpos // CHUNK, _t5_bucket(kpos - qpos), -1)
    buckets = jnp.stack([prev, diag])
    assert DA_TK >= REL_MAX_DIST and DA_TQ == DA_TK
    far_bucket = REL_BUCKETS // 2 - 1
    return pl.pallas_call(
        functools.partial(_bias_tiles_kernel, far_bucket=far_bucket),
        grid=(DA_HEADS,),
        in_specs=[
            pl.BlockSpec(memory_space=pltpu.SMEM),
            pl.BlockSpec((2, DA_TK, DA_TQ), lambda h: (0, 0, 0)),
        ],
        out_specs=pl.BlockSpec((None, 2, DA_TK, DA_TQ), lambda h: (h, 0, 0, 0)),
        out_shape=jax.ShapeDtypeStruct((DA_HEADS, 2, DA_TK, DA_TQ), F32),
        compiler_params=_params(("parallel",)),
        name="bias_tiles",
    )(rel_bias, buckets)


def _da_kernel(q_ref, k_ref, v_ref, bias_ref, lam_ref, sg_ref, o_ref,
               vt_ref, m_ref, l_ref, acc_ref, s_ref, *, lam_init):
    i = pl.program_id(1)
    tq, tk = DA_TQ, DA_TK
    blk = 2 * DA_DIM
    chains = [(h, c) for h in range(DA_HEADS) for c in range(2)]

    @pl.when(i == 0)
    def _():
        ones = jnp.ones((DA_ONES_ROWS, 2 * tk), BF16)
        for h in range(DA_HEADS):
            for jp in range(vt_ref.shape[1]):
                for half in range(2):
                    r0 = (2 * jp + half) * tk
                    v = v_ref[r0:r0 + tk, h * DA_VDIM:(h + 1) * DA_VDIM]
                    vt_ref[h, jp, 0:DA_VDIM, half * tk:(half + 1) * tk] = v.astype(F32).T.astype(BF16)
                vt_ref[h, jp, DA_VDIM:DA_VDIM + DA_ONES_ROWS, :] = ones

    lane = lax.broadcasted_iota(jnp.int32, (tq, blk), 1)
    qz = []
    for h, c in chains:
        q = q_ref[:, h * blk:(h + 1) * blk]
        qz.append(jnp.where((lane >= DA_DIM) == (c == 1), q, jnp.zeros_like(q)))

    m_ref[...] = jnp.full(m_ref.shape, MASK_VALUE, F32)
    l_ref[...] = jnp.zeros(l_ref.shape, F32)
    acc_ref[...] = jnp.zeros(acc_ref.shape, F32)

    def step(jp, nears):
        nkeys = len(nears) * tk
        k0 = pl.multiple_of(jp * (2 * tk), 2 * tk)

        def logits(n):
            h = chains[n][0]
            kb = k_ref[pl.ds(k0, nkeys), h * blk:(h + 1) * blk]
            s = lax.dot_general(kb, qz[n], (((1,), (1,)), ((), ())), preferred_element_type=F32)
            if all(near is None for near in nears):
                return s
            parts = [s[t * tk:(t + 1) * tk, :] if near is None else s[t * tk:(t + 1) * tk, :] + bias_ref[h, near]
                     for t, near in enumerate(nears)]
            return jnp.concatenate(parts, axis=0)

        block_max = []
        for n in range(len(chains)):
            s = logits(n)
            s_ref[n, 0:nkeys, :] = s
            block_max.append(jnp.max(s, axis=0, keepdims=True))
        alphas, probs = [], []
        for n in range(len(chains)):
            m_prev = m_ref[n]
            m_new = jnp.maximum(m_prev, block_max[n])
            alphas.append(jnp.exp2(m_prev - m_new))
            probs.append(jnp.exp2(s_ref[n, 0:nkeys, :] - m_new).astype(BF16))
            m_ref[n] = m_new
        for n, (h, c) in enumerate(chains):
            pv = jnp.dot(vt_ref[h, jp, :, 0:nkeys], probs[n], preferred_element_type=F32)
            acc_ref[n] = alphas[n] * acc_ref[n] + pv[0:DA_VDIM, :]
            l_ref[n] = alphas[n] * l_ref[n] + pv[DA_VDIM:DA_VDIM + 1, :]

    def far_body(jp, carry):
        step(jp, (None, None))
        return carry

    lax.fori_loop(0, jnp.maximum(i - 1, 0) // 2, far_body, 0)
    odd = (i % 2) == 1

    @pl.when(odd)
    def _():
        step((i - 1) // 2, (0, 1))

    @pl.when(jnp.logical_and(jnp.logical_not(odd), i >= 2))
    def _():
        step((i - 2) // 2, (None, 0))

    @pl.when(jnp.logical_not(odd))
    def _():
        step(i // 2, (1,))

    lv = lam_ref[...]
    lam = (jnp.exp(jnp.sum(lv[0:1, :] * lv[1:2, :], axis=-1, keepdims=True))
           - jnp.exp(jnp.sum(lv[2:3, :] * lv[3:4, :], axis=-1, keepdims=True)) + lam_init)
    for h in range(DA_HEADS):
        d = acc_ref[2 * h] / l_ref[2 * h] - lam * (acc_ref[2 * h + 1] / l_ref[2 * h + 1])
        y = d * lax.rsqrt(jnp.mean(d * d, axis=0, keepdims=True) + EPS)
        o_ref[:, h * DA_VDIM:(h + 1) * DA_VDIM] = (y.T * sg_ref[...] * (1.0 - lam_init)).astype(o_ref.dtype)


def _diff_attention(zmix, batch, seq, bias_tiles, lam_vec, subln_g, lam_init):
    nq = seq // DA_TQ
    assert COL_Q % D_DA == 0 and COL_K % D_DA == 0 and COL_V % D_DA == 0 and seq % (2 * DA_TK) == 0
    return pl.pallas_call(
        functools.partial(_da_kernel, lam_init=lam_init),
        grid=(batch, nq),
        in_specs=[
            pl.BlockSpec((DA_TQ, D_DA), lambda b, i: (b * nq + i, COL_Q // D_DA)),
            pl.BlockSpec((seq, D_DA), lambda b, i: (b, COL_K // D_DA)),
            pl.BlockSpec((seq, D_DA), lambda b, i: (b, COL_V // D_DA)),
            _resident((DA_HEADS, 2, DA_TK, DA_TQ)),
            _resident((4, DA_DIM)),
            _resident((1, DA_VDIM)),
        ],
        out_specs=pl.BlockSpec((DA_TQ, D_DA), lambda b, i: (b * nq + i, 0)),
        out_shape=jax.ShapeDtypeStruct((batch * seq, D_DA), BF16),
        scratch_shapes=[
            pltpu.VMEM((DA_HEADS, seq // (2 * DA_TK), DA_VDIM + DA_ONES_ROWS, 2 * DA_TK), BF16),
            pltpu.VMEM((2 * DA_HEADS, 1, DA_TQ), F32),
            pltpu.VMEM((2 * DA_HEADS, 1, DA_TQ), F32),
            pltpu.VMEM((2 * DA_HEADS, DA_VDIM, DA_TQ), F32),
            pltpu.VMEM((2 * DA_HEADS, 2 * DA_TK, DA_TQ), F32),
        ],
        compiler_params=_params(("parallel", "arbitrary")),
        name="diff_attention",
    )(zmix, zmix, zmix, bias_tiles, lam_vec, subln_g)


def _merge_xattn_kernel(x_ref, ya_ref, yb_ref, yc_ref, g0_ref, g1_ref, g2_ref,
                        wa_ref, wb_ref, wc_ref, wo_ref, gx_ref, wq_ref, kv_ref, wxo_ref, o_ref, oc_ref):
    merged = (g0_ref[...].astype(F32) * jnp.dot(ya_ref[...], wa_ref[...], preferred_element_type=F32)
              + g1_ref[...].astype(F32) * jnp.dot(yb_ref[...], wb_ref[...], preferred_element_type=F32)
              + g2_ref[...].astype(F32) * jnp.dot(yc_ref[...], wc_ref[...], preferred_element_type=F32))
    x = x_ref[...] + jnp.dot(merged.astype(BF16), wo_ref[...], preferred_element_type=F32)

    hq = _rms_norm(x, gx_ref[...]).astype(BF16)
    q = (jnp.dot(hq, wq_ref[...], preferred_element_type=F32) * (XA_DIM ** -0.5)).astype(BF16)
    for h in range(XA_HEADS):
        c0 = h * XA_DIM
        kh = kv_ref[:, c0:c0 + XA_DIM]
        vh = kv_ref[:, D_MODEL + c0:D_MODEL + c0 + XA_DIM]
        s = lax.dot_general(q[:, c0:c0 + XA_DIM], kh, (((1,), (1,)), ((), ())), preferred_element_type=F32)
        p = jnp.exp(s - jnp.max(s, axis=-1, keepdims=True))
        l = jnp.sum(p, axis=-1, keepdims=True)
        o = jnp.dot(p.astype(BF16), vh, preferred_element_type=F32) / l
        oc_ref[:, c0:c0 + XA_DIM] = o.astype(BF16)
    o_ref[...] = x + jnp.dot(oc_ref[...], wxo_ref[...], preferred_element_type=F32)


def _merge_xattn(x2d, batch, seq, ya, yb, yc, zmix, wa, wb, wc, wo, gx, wq, kv, wxo):
    nt = seq // ROW_TILE
    n_mem = kv.shape[0] // batch
    tile = lambda width, col=0: pl.BlockSpec((ROW_TILE, width), lambda b, t: (b * nt + t, col))
    return pl.pallas_call(
        _merge_xattn_kernel,
        grid=(batch, nt),
        in_specs=[
            tile(D_MODEL),
            tile(D_LRU),
            tile(D_CONV),
            tile(D_DA),
            tile(D_MODEL, 0),
            tile(D_MODEL, 1),
            tile(D_MODEL, 2),
            _resident((D_LRU, D_MODEL)),
            _resident((D_CONV, D_MODEL)),
            _resident((D_DA, D_MODEL)),
            _resident((D_MODEL, D_MODEL)),
            _resident((1, D_MODEL)),
            _resident((D_MODEL, D_MODEL)),
            pl.BlockSpec((n_mem, 2 * D_MODEL), lambda b, t: (b, 0)),
            _resident((D_MODEL, D_MODEL)),
        ],
        out_specs=tile(D_MODEL),
        out_shape=jax.ShapeDtypeStruct(x2d.shape, F32),
        scratch_shapes=[pltpu.VMEM((ROW_TILE, D_MODEL), BF16)],
        compiler_params=_params(("parallel", "arbitrary")),
        name="merge_xattn",
    )(x2d, ya, yb, yc, zmix, zmix, zmix, wa, wb, wc, wo, gx, wq, kv, wxo)


def _ffn_kernel(x_ref, g_ref, w1_ref, w3_ref, cw_ref, cb_ref, w2_ref, fg_ref, o_ref,
                ae_ref, halo_ref, hm_ref, *, final_norm):
    ts = x_ref.shape[0]
    halo = SUBLANES

    @pl.when(pl.program_id(1) == 0)
    def _():
        halo_ref[...] = jnp.zeros(halo_ref.shape, F32)

    x = x_ref[...]
    hf = _rms_norm(x, g_ref[...]).astype(BF16)
    for c0 in range(0, D_FF, FFN_COL_CHUNK):
        cs = slice(c0, c0 + FFN_COL_CHUNK)
        a = jnp.dot(hf, w1_ref[:, cs], preferred_element_type=F32)
        ae_ref[0:halo, :] = halo_ref[:, cs]
        ae_ref[halo:halo + ts, :] = a
        halo_ref[:, cs] = a[ts - halo:ts, :]
        y = cb_ref[:, cs] + cw_ref[FFN_CONV - 1:FFN_CONV, cs] * a
        for j in range(FFN_CONV - 1):
            off = halo - (FFN_CONV - 1) + j
            y = y + cw_ref[j:j + 1, cs] * ae_ref[off:off + ts, :]
        up = jnp.dot(hf, w3_ref[:, cs], preferred_element_type=F32)
        hm_ref[:, cs] = (y * _sigmoid(y) * up).astype(BF16)
    out = x + jnp.dot(hm_ref[...], w2_ref[...], preferred_element_type=F32)
    if final_norm:
        out = _rms_norm(out, fg_ref[...])
    o_ref[...] = out


def _ffn(x2d, batch, seq, g, w1, w3, cw, cb, w2, fg, final_norm):
    nt = seq // SEQ_TILE
    return pl.pallas_call(
        functools.partial(_ffn_kernel, final_norm=final_norm),
        grid=(batch, nt),
        in_specs=[
            pl.BlockSpec((SEQ_TILE, D_MODEL), lambda b, t: (b * nt + t, 0)),
            _resident((1, D_MODEL)),
            _resident((D_MODEL, D_FF)),
            _resident((D_MODEL, D_FF)),
            _resident((FFN_CONV, D_FF)),
            _resident((1, D_FF)),
            _resident((D_FF, D_MODEL)),
            _resident((1, D_MODEL)),
        ],
        out_specs=pl.BlockSpec((SEQ_TILE, D_MODEL), lambda b, t: (b * nt + t, 0)),
        out_shape=jax.ShapeDtypeStruct(x2d.shape, F32),
        scratch_shapes=[
            pltpu.VMEM((SEQ_TILE + SUBLANES, FFN_COL_CHUNK), F32),
            pltpu.VMEM((SUBLANES, D_FF), F32),
            pltpu.VMEM((SEQ_TILE, D_FF), BF16),
        ],
        compiler_params=_params(("parallel", "arbitrary")),
        name="ffn_final" if final_norm else "ffn",
    )(x2d, g, w1, w3, cw, cb, w2, fg)


def _block_diag(w):
    nb, bi, bj = w.shape
    eye = jnp.eye(nb, dtype=w.dtype)
    return jnp.einsum('hij,hg->higj', w, eye).reshape(nb * bi, nb * bj)


def kernel(x, mem, rel_bias, norm_mix_g, w_in, w_gate, b_gate, lru_conv_w, lru_conv_b, lru_wr, lru_br, lru_wi, lru_bi, lru_lambda, lru_out, cm_conv_w, cm_conv_b, cm_ln_g, cm_ln_b, cm_out, da_lambda, da_subln_g, da_out, w_o, norm_xa_g, norm_mem_g, xa_wq, xa_wkv, xa_wo, norm_ffn_g, ffn_w1, ffn_w3, ffn_conv_w, ffn_conv_b, ffn_w2, final_g):
    batch, seq, d = x.shape
    n_mem = mem.shape[1]
    depth = w_in.shape[0]
    assert d == D_MODEL and seq % SEQ_TILE == 0 and seq % ROW_TILE == 0 and seq % DA_TQ == 0
    assert w_in.shape[2] == D_IN

    bias_tiles = _bias_tiles(rel_bias)
    x2d = x.reshape(batch * seq, d)
    mem2d = mem.reshape(batch * n_mem, d)
    row = lambda v: v.reshape(1, -1)
    no_bias = jnp.zeros((1, MIX_COL_CHUNK), F32)

    for l in range(depth):
        w_mix = jnp.concatenate([w_gate[l, 0], w_gate[l, 1], w_gate[l, 2], w_in[l, :, IN_Q:], w_in[l, :, :IN_Q]],
                                axis=1).astype(BF16)
        zmix = _norm_proj(x2d, row(norm_mix_g[l]), w_mix, row(b_gate[l]), "mix_proj",
                          n_sig=N_GATE, scaled=(COL_Q, COL_K, DA_Q_SCALE))
        wri = jnp.concatenate([_block_diag(lru_wr[l]), _block_diag(lru_wi[l])], axis=1).astype(BF16)
        bri = jnp.concatenate([lru_br[l], lru_bi[l]]).reshape(1, -1)
        ya = _lru_branch(zmix, batch, seq, lru_conv_w[l], row(lru_conv_b[l]), wri, bri, row(lru_lambda[l]))
        yb = _cconv_branch(zmix, batch, seq, cm_conv_w[l], row(cm_conv_b[l]), row(cm_ln_g[l]), row(cm_ln_b[l]))
        lam_init = 0.8 - 0.6 * math.exp(-0.3 * l)
        yc = _diff_attention(zmix, batch, seq, bias_tiles, da_lambda[l], row(da_subln_g[l]), lam_init)
        kv = _norm_proj(mem2d, row(norm_mem_g[l]), xa_wkv[l].astype(BF16), no_bias, "mem_kv")
        x2d = _merge_xattn(x2d, batch, seq, ya, yb, yc, zmix,
                           lru_out[l].astype(BF16), cm_out[l].astype(BF16), da_out[l].astype(BF16),
                           w_o[l].astype(BF16), row(norm_xa_g[l]), xa_wq[l].astype(BF16), kv,
                           xa_wo[l].astype(BF16))
        x2d = _ffn(x2d, batch, seq, row(norm_ffn_g[l]), ffn_w1[l].astype(BF16), ffn_w3[l].astype(BF16),
                   ffn_conv_w[l], row(ffn_conv_b[l]), ffn_w2[l].astype(BF16), row(final_g),
                   final_norm=(l == depth - 1))
    return x2d.reshape(batch, seq, d)
```

```python
import functools
import math

import jax
import jax.numpy as jnp
from jax import lax
from jax.experimental import pallas as pl
from jax.experimental.pallas import tpu as pltpu

F32 = jnp.float32
BF16 = jnp.bfloat16

D_MODEL = 1024
CHUNK = 64
D_LRU = 512
LRU_BLOCKS = 8
LRU_CONV = 4
LRU_C = 8.0
D_CONV = 512
CONV_WIDTH = 31
DA_HEADS = 4
DA_DIM = 64
DA_VDIM = 2 * DA_DIM
D_DA = DA_HEADS * DA_VDIM
REL_BUCKETS = 32
REL_MAX_DIST = 128
XA_HEADS = 4
XA_DIM = D_MODEL // XA_HEADS
D_FF = 2816
FFN_CONV = 3
N_BRANCH = 3
EPS = 1e-6

IN_LRU_X = 0
IN_LRU_G = IN_LRU_X + D_LRU
IN_CONV_A = IN_LRU_G + D_LRU
IN_CONV_G = IN_CONV_A + D_CONV
IN_Q = IN_CONV_G + D_CONV
IN_K = IN_Q + DA_HEADS * 2 * DA_DIM
IN_V = IN_K + DA_HEADS * 2 * DA_DIM
D_IN = IN_V + D_DA
N_GATE = N_BRANCH * D_MODEL
COL_Q = N_GATE
COL_K = COL_Q + DA_HEADS * 2 * DA_DIM
COL_V = COL_K + DA_HEADS * 2 * DA_DIM
COL_LRU_X = COL_V + D_DA
COL_LRU_G = COL_LRU_X + D_LRU
COL_CONV_A = COL_LRU_G + D_LRU
COL_CONV_G = COL_CONV_A + D_CONV
N_MIX = COL_CONV_G + D_CONV

SUBLANES = 8
LANES = 128
VMEM_LIMIT = 56 * 1024 * 1024
ROW_TILE = 512
MIX_COL_CHUNK = 512
SEQ_TILE = 512
FFN_TILE = 2 * SEQ_TILE
MERGE_TILE = 2 * ROW_TILE
CONV_ROWS = 32
LRU_ROWS = 32
CONV_HALO = 32
CONV_FIRST_TAP = CONV_HALO - (CONV_WIDTH - 1)
DA_TQ = 256
DA_TK = 256
FFN_COL_CHUNK = 256
DA_ONES_ROWS = 16
MASK_VALUE = -1e30
LOG2E = math.log2(math.e)
DA_Q_SCALE = DA_DIM ** -0.5 * LOG2E


def _resident(shape):
    nd = len(shape)
    return pl.BlockSpec(shape, lambda *_: (0,) * nd, pipeline_mode=pl.Buffered(1))


def _params(semantics):
    return pltpu.CompilerParams(dimension_semantics=semantics, vmem_limit_bytes=VMEM_LIMIT)


def _rms_norm(x, g):
    return x * lax.rsqrt(jnp.mean(x * x, axis=-1, keepdims=True) + EPS) * g


def _sigmoid(x):
    return 1.0 / (1.0 + jnp.exp(-x))


def _gelu_tanh(x):
    c = math.sqrt(2.0 / math.pi)
    return 0.5 * x * (1.0 + jnp.tanh(c * (x + 0.044715 * (x * x * x))))


def _norm_proj_kernel(x_ref, g_ref, w_ref, b_ref, o_ref, *, n_sig, scaled):
    h = _rms_norm(x_ref[...], g_ref[...]).astype(BF16)
    for c0 in range(0, o_ref.shape[1], MIX_COL_CHUNK):
        y = jnp.dot(h, w_ref[:, c0:c0 + MIX_COL_CHUNK], preferred_element_type=F32)
        if c0 < n_sig:
            y = _sigmoid(y + b_ref[:, c0:c0 + MIX_COL_CHUNK])
        elif scaled is not None and scaled[0] <= c0 < scaled[1]:
            y = y * scaled[2]
        o_ref[:, c0:c0 + MIX_COL_CHUNK] = y.astype(o_ref.dtype)


def _norm_proj(x2d, g, w, b, name, n_sig=0, scaled=None):
    n, d = x2d.shape
    n_out = w.shape[1]
    assert n % ROW_TILE == 0 and n_out % MIX_COL_CHUNK == 0 and n_sig % MIX_COL_CHUNK == 0
    assert scaled is None or (scaled[0] % MIX_COL_CHUNK == 0 and scaled[1] % MIX_COL_CHUNK == 0)
    return pl.pallas_call(
        functools.partial(_norm_proj_kernel, n_sig=n_sig, scaled=scaled),
        grid=(n // ROW_TILE,),
        in_specs=[
            pl.BlockSpec((ROW_TILE, d), lambda i: (i, 0)),
            _resident((1, d)),
            _resident((d, n_out)),
            _resident((1, b.shape[1])),
        ],
        out_specs=pl.BlockSpec((ROW_TILE, n_out), lambda i: (i, 0)),
        out_shape=jax.ShapeDtypeStruct((n, n_out), BF16),
        compiler_params=_params(("parallel",)),
        name=name,
    )(x2d, g, w, b)


def _lru_rows(r0, carry, softplus, xa_ref, ri_ref, zg_ref, o_ref):
    rows = slice(r0, r0 + LRU_ROWS)
    xa = xa_ref[rows, :]
    r = _sigmoid(ri_ref[rows, 0:D_LRU])
    gi = _sigmoid(ri_ref[rows, D_LRU:2 * D_LRU])
    a = jnp.exp((-LRU_C) * r * softplus)
    u = jnp.sqrt(1.0 - a * a) * (gi * xa)
    row = lax.broadcasted_iota(jnp.int32, (LRU_ROWS, D_LRU), 0) & (SUBLANES - 1)
    s = 1
    while s < SUBLANES:
        keep = row >= s
        u = jnp.where(keep, a * pltpu.roll(u, s, 0) + u, u)
        a = jnp.where(keep, a * pltpu.roll(a, s, 0), a)
        s *= 2
    hs = []
    for g0 in range(0, LRU_ROWS, SUBLANES):
        h = a[g0:g0 + SUBLANES, :] * carry + u[g0:g0 + SUBLANES, :]
        hs.append(h)
        carry = jnp.broadcast_to(h[SUBLANES - 1:SUBLANES, :], (SUBLANES, D_LRU))
    gate = _gelu_tanh(zg_ref[rows, :].astype(F32))
    o_ref[rows, :] = (jnp.concatenate(hs, axis=0) * gate).astype(o_ref.dtype)
    return carry


def _lru_kernel(zx_ref, zg_ref, cw_ref, cb_ref, wri_ref, bri_ref, lam_ref, o_ref,
                xe_ref, xa_ref, ri_ref, carry_ref, sp_ref):
    ts = zx_ref.shape[0]
    halo = SUBLANES

    @pl.when(pl.program_id(1) == 0)
    def _():
        xe_ref[0:halo, :] = jnp.zeros((halo, D_LRU), F32)
        carry_ref[...] = jnp.zeros((SUBLANES, D_LRU), F32)
        nlam = -lam_ref[...]
        sp_ref[...] = jnp.maximum(nlam, 0.0) + jnp.log(1.0 + jnp.exp(-jnp.abs(nlam)))

    xe_ref[halo:halo + ts, :] = zx_ref[...].astype(F32)
    xa = cb_ref[...] + cw_ref[LRU_CONV - 1:LRU_CONV, :] * xe_ref[halo:halo + ts, :]
    for j in range(LRU_CONV - 1):
        off = halo - (LRU_CONV - 1) + j
        xa = xa + cw_ref[j:j + 1, :] * xe_ref[off:off + ts, :]
    xe_ref[0:halo, :] = xe_ref[ts:ts + halo, :]
    xa_ref[...] = xa
    ri_ref[...] = jnp.dot(xa.astype(BF16), wri_ref[...], preferred_element_type=F32) + bri_ref[...]

    carry = carry_ref[...]
    for r0 in range(0, ts, LRU_ROWS):
        carry = _lru_rows(r0, carry, sp_ref[...], xa_ref, ri_ref, zg_ref, o_ref)
    carry_ref[...] = carry


def _lru_branch(zmix, batch, seq, cw, cb, wri, bri, lam):
    nt = seq // SEQ_TILE
    return pl.pallas_call(
        _lru_kernel,
        grid=(batch, nt),
        in_specs=[
            pl.BlockSpec((SEQ_TILE, D_LRU), lambda b, t: (b * nt + t, COL_LRU_X // D_LRU)),
            pl.BlockSpec((SEQ_TILE, D_LRU), lambda b, t: (b * nt + t, COL_LRU_G // D_LRU)),
            _resident((LRU_CONV, D_LRU)),
            _resident((1, D_LRU)),
            _resident((D_LRU, 2 * D_LRU)),
            _resident((1, 2 * D_LRU)),
            _resident((1, D_LRU)),
        ],
        out_specs=pl.BlockSpec((SEQ_TILE, D_LRU), lambda b, t: (b * nt + t, 0)),
        out_shape=jax.ShapeDtypeStruct((batch * seq, D_LRU), BF16),
        scratch_shapes=[
            pltpu.VMEM((SEQ_TILE + SUBLANES, D_LRU), F32),
            pltpu.VMEM((SEQ_TILE, D_LRU), F32),
            pltpu.VMEM((SEQ_TILE, 2 * D_LRU), F32),
            pltpu.VMEM((SUBLANES, D_LRU), F32),
            pltpu.VMEM((1, D_LRU), F32),
        ],
        compiler_params=_params(("parallel", "arbitrary")),
        name="lru_branch",
    )(zmix, zmix, cw, cb, wri, bri, lam)


def _cconv_rows(r0, b_ref, lg_ref, lb_ref, o_ref, ce_ref, sh_ref, wb_ref):
    groups = range(0, CONV_ROWS, SUBLANES)
    accs = [jnp.broadcast_to(b_ref[...], (SUBLANES, D_CONV)) for _ in groups]
    for j in range(CONV_WIDTH):
        r = (CONV_FIRST_TAP + j) % SUBLANES
        base = r0 + CONV_FIRST_TAP + j - r
        wb = wb_ref[j]
        for k, g0 in enumerate(groups):
            lo = base + g0
            src = ce_ref[lo:lo + SUBLANES, :] if r == 0 else sh_ref[r - 1, lo:lo + SUBLANES, :]
            accs[k] = accs[k] + wb * src
    acc = jnp.concatenate(accs, axis=0)
    mu = jnp.mean(acc, axis=-1, keepdims=True)
    xc = acc - mu
    y = xc * lax.rsqrt(jnp.mean(xc * xc, axis=-1, keepdims=True) + EPS) * lg_ref[...] + lb_ref[...]
    o_ref[r0:r0 + CONV_ROWS, :] = (y * _sigmoid(y)).astype(o_ref.dtype)


def _cconv_kernel(ca_ref, cg_ref, w_ref, b_ref, lg_ref, lb_ref, o_ref, ce_ref, sh_ref, wb_ref):
    ts = ca_ref.shape[0]

    @pl.when(pl.program_id(1) == 0)
    def _():
        ce_ref[0:CONV_HALO, :] = jnp.zeros((CONV_HALO, D_CONV), F32)
        for j in range(CONV_WIDTH):
            wb_ref[j] = jnp.broadcast_to(w_ref[j:j + 1, :], (SUBLANES, D_CONV))

    ce_ref[CONV_HALO:CONV_HALO + ts, :] = ca_ref[...].astype(F32) * _sigmoid(cg_ref[...].astype(F32))
    for r in range(1, SUBLANES):
        sh_ref[r - 1] = ce_ref[r:r + sh_ref.shape[1], :]
    for r0 in range(0, ts, CONV_ROWS):
        _cconv_rows(r0, b_ref, lg_ref, lb_ref, o_ref, ce_ref, sh_ref, wb_ref)
    ce_ref[0:CONV_HALO, :] = ce_ref[ts:ts + CONV_HALO, :]


def _cconv_branch(zmix, batch, seq, w, b, lg, lb):
    nt = seq // SEQ_TILE
    return pl.pallas_call(
        _cconv_kernel,
        grid=(batch, nt),
        in_specs=[
            pl.BlockSpec((SEQ_TILE, D_CONV), lambda b_, t: (b_ * nt + t, COL_CONV_A // D_CONV)),
            pl.BlockSpec((SEQ_TILE, D_CONV), lambda b_, t: (b_ * nt + t, COL_CONV_G // D_CONV)),
            _resident((CONV_WIDTH, D_CONV)),
            _resident((1, D_CONV)),
            _resident((1, D_CONV)),
            _resident((1, D_CONV)),
        ],
        out_specs=pl.BlockSpec((SEQ_TILE, D_CONV), lambda b_, t: (b_ * nt + t, 0)),
        out_shape=jax.ShapeDtypeStruct((batch * seq, D_CONV), BF16),
        scratch_shapes=[
            pltpu.VMEM((SEQ_TILE + CONV_HALO, D_CONV), F32),
            pltpu.VMEM((SUBLANES - 1, SEQ_TILE + CONV_HALO - SUBLANES, D_CONV), F32),
            pltpu.VMEM((CONV_WIDTH, SUBLANES, D_CONV), F32),
        ],
        compiler_params=_params(("parallel", "arbitrary")),
        name="cconv_branch",
    )(zmix, zmix, w, b, lg, lb)


def _t5_bucket(rel):
    nb = REL_BUCKETS // 2
    ret = jnp.where(rel > 0, nb, 0)
    n = jnp.abs(rel)
    max_exact = nb // 2
    large = max_exact + (jnp.log(jnp.maximum(n, 1).astype(F32) / max_exact)
                         / math.log(REL_MAX_DIST / max_exact) * (nb - max_exact)).astype(jnp.int32)
    large = jnp.minimum(large, nb - 1)
    return ret + jnp.where(n < max_exact, n, large)


def _bias_tiles_kernel(tbl_ref, bucket_ref, o_ref, *, far_bucket):
    h = pl.program_id(0)
    bucket = bucket_ref[...]
    acc = jnp.zeros(bucket.shape, F32)
    for bkt in range(REL_BUCKETS):
        acc = jnp.where(bucket == bkt, tbl_ref[bkt, h], acc)
    acc = (acc - tbl_ref[far_bucket, h]) * LOG2E
    o_ref[...] = jnp.where(bucket < 0, MASK_VALUE, acc)


def _bias_tiles(rel_bias):
    kpos = jnp.arange(DA_TK, dtype=jnp.int32)[:, None]
    qpos = jnp.arange(DA_TQ, dtype=jnp.int32)[None, :]
    prev = _t5_bucket(kpos - DA_TK - qpos)
    diag = jnp.where(kpos // CHUNK <= qpos // CHUNK, _t5_bucket(kpos - qpos), -1)
    buckets = jnp.stack([prev, diag])
    assert DA_TK >= REL_MAX_DIST and DA_TQ == DA_TK
    far_bucket = REL_BUCKETS // 2 - 1
    return pl.pallas_call(
        functools.partial(_bias_tiles_kernel, far_bucket=far_bucket),
        grid=(DA_HEADS,),
        in_specs=[
            pl.BlockSpec(memory_space=pltpu.SMEM),
            pl.BlockSpec((2, DA_TK, DA_TQ), lambda h: (0, 0, 0)),
        ],
        out_specs=pl.BlockSpec((None, 2, DA_TK, DA_TQ), lambda h: (h, 0, 0, 0)),
        out_shape=jax.ShapeDtypeStruct((DA_HEADS, 2, DA_TK, DA_TQ), F32),
        compiler_params=_params(("parallel",)),
        name="bias_tiles",
    )(rel_bias, buckets)


def _da_kernel(q_ref, k_ref, v_ref, bias_ref, lam_ref, sg_ref, o_ref,
               vt_ref, m_ref, l_ref, acc_ref, s_ref, *, lam_init):
    tq, tk = DA_TQ, DA_TK
    blk = 2 * DA_DIM
    chains = [(h, c) for h in range(DA_HEADS) for c in range(2)]

    ones = jnp.ones((DA_ONES_ROWS, 2 * tk), BF16)
    for h in range(DA_HEADS):
        for jp in range(vt_ref.shape[1]):
            for half in range(2):
                r0 = (2 * jp + half) * tk
                v = v_ref[r0:r0 + tk, h * DA_VDIM:(h + 1) * DA_VDIM]
                vt_ref[h, jp, 0:DA_VDIM, half * tk:(half + 1) * tk] = v.astype(F32).T.astype(BF16)
            vt_ref[h, jp, DA_VDIM:DA_VDIM + DA_ONES_ROWS, :] = ones

    lv = lam_ref[...]
    lam = (jnp.exp(jnp.sum(lv[0:1, :] * lv[1:2, :], axis=-1, keepdims=True))
           - jnp.exp(jnp.sum(lv[2:3, :] * lv[3:4, :], axis=-1, keepdims=True)) + lam_init)
    lane = lax.broadcasted_iota(jnp.int32, (tq, blk), 1)

    def query_tile(i, carry):
        _da_query_tile(i, lam, lane, chains, q_ref, k_ref, bias_ref, sg_ref, o_ref,
                       vt_ref, m_ref, l_ref, acc_ref, s_ref, lam_init)
        return carry

    lax.fori_loop(0, q_ref.shape[0] // tq, query_tile, 0)


def _da_query_tile(i, lam, lane, chains, q_ref, k_ref, bias_ref, sg_ref, o_ref,
                   vt_ref, m_ref, l_ref, acc_ref, s_ref, lam_init):
    tq, tk = DA_TQ, DA_TK
    blk = 2 * DA_DIM
    q0 = pl.multiple_of(i * tq, tq)
    qz = []
    for h, c in chains:
        q = q_ref[pl.ds(q0, tq), h * blk:(h + 1) * blk]
        qz.append(jnp.where((lane >= DA_DIM) == (c == 1), q, jnp.zeros_like(q)))

    m_ref[...] = jnp.full(m_ref.shape, MASK_VALUE, F32)
    l_ref[...] = jnp.zeros(l_ref.shape, F32)
    acc_ref[...] = jnp.zeros(acc_ref.shape, F32)

    def step(jp, nears):
        nkeys = len(nears) * tk
        k0 = pl.multiple_of(jp * (2 * tk), 2 * tk)

        def logits(n):
            h = chains[n][0]
            kb = k_ref[pl.ds(k0, nkeys), h * blk:(h + 1) * blk]
            s = lax.dot_general(kb, qz[n], (((1,), (1,)), ((), ())), preferred_element_type=F32)
            if all(near is None for near in nears):
                return s
            parts = [s[t * tk:(t + 1) * tk, :] if near is None else s[t * tk:(t + 1) * tk, :] + bias_ref[h, near]
                     for t, near in enumerate(nears)]
            return jnp.concatenate(parts, axis=0)

        block_max = []
        for n in range(len(chains)):
            s = logits(n)
            s_ref[n, 0:nkeys, :] = s
            block_max.append(jnp.max(s, axis=0, keepdims=True))
        alphas, probs = [], []
        for n in range(len(chains)):
            m_prev = m_ref[n]
            m_new = jnp.maximum(m_prev, block_max[n])
            alphas.append(jnp.exp2(m_prev - m_new))
            probs.append(jnp.exp2(s_ref[n, 0:nkeys, :] - m_new).astype(BF16))
            m_ref[n] = m_new
        for n, (h, c) in enumerate(chains):
            pv = jnp.dot(vt_ref[h, jp, :, 0:nkeys], probs[n], preferred_element_type=F32)
            acc_ref[n] = alphas[n] * acc_ref[n] + pv[0:DA_VDIM, :]
            l_ref[n] = alphas[n] * l_ref[n] + pv[DA_VDIM:DA_VDIM + 1, :]

    def far_body(jp, carry):
        step(jp, (None, None))
        return carry

    lax.fori_loop(0, jnp.maximum(i - 1, 0) // 2, far_body, 0)
    odd = (i % 2) == 1

    @pl.when(odd)
    def _():
        step((i - 1) // 2, (0, 1))

    @pl.when(jnp.logical_and(jnp.logical_not(odd), i >= 2))
    def _():
        step((i - 2) // 2, (None, 0))

    @pl.when(jnp.logical_not(odd))
    def _():
        step(i // 2, (1,))

    for h in range(DA_HEADS):
        d = (acc_ref[2 * h] * (1.0 / l_ref[2 * h])
             - lam * (acc_ref[2 * h + 1] * (1.0 / l_ref[2 * h + 1])))
        y = d * lax.rsqrt(jnp.mean(d * d, axis=0, keepdims=True) + EPS)
        o_ref[pl.ds(q0, tq), h * DA_VDIM:(h + 1) * DA_VDIM] = (
            y.T * sg_ref[...] * (1.0 - lam_init)).astype(o_ref.dtype)


def _diff_attention(zmix, batch, seq, bias_tiles, lam_vec, subln_g, lam_init):
    assert COL_Q % D_DA == 0 and COL_K % D_DA == 0 and COL_V % D_DA == 0
    assert seq % (2 * DA_TK) == 0 and seq % DA_TQ == 0
    return pl.pallas_call(
        functools.partial(_da_kernel, lam_init=lam_init),
        grid=(batch,),
        in_specs=[
            pl.BlockSpec((seq, D_DA), lambda b: (b, COL_Q // D_DA)),
            pl.BlockSpec((seq, D_DA), lambda b: (b, COL_K // D_DA)),
            pl.BlockSpec((seq, D_DA), lambda b: (b, COL_V // D_DA)),
            _resident((DA_HEADS, 2, DA_TK, DA_TQ)),
            _resident((4, DA_DIM)),
            _resident((1, DA_VDIM)),
        ],
        out_specs=pl.BlockSpec((seq, D_DA), lambda b: (b, 0)),
        out_shape=jax.ShapeDtypeStruct((batch * seq, D_DA), BF16),
        scratch_shapes=[
            pltpu.VMEM((DA_HEADS, seq // (2 * DA_TK), DA_VDIM + DA_ONES_ROWS, 2 * DA_TK), BF16),
            pltpu.VMEM((2 * DA_HEADS, 1, DA_TQ), F32),
            pltpu.VMEM((2 * DA_HEADS, 1, DA_TQ), F32),
            pltpu.VMEM((2 * DA_HEADS, DA_VDIM, DA_TQ), F32),
            pltpu.VMEM((2 * DA_HEADS, 2 * DA_TK, DA_TQ), F32),
        ],
        compiler_params=_params(("parallel",)),
        name="diff_attention",
    )(zmix, zmix, zmix, bias_tiles, lam_vec, subln_g)


def _merge_xattn_kernel(x_ref, ya_ref, yb_ref, yc_ref, g0_ref, g1_ref, g2_ref,
                        wa_ref, wb_ref, wc_ref, wo_ref, gx_ref, wq_ref, kv_ref, wxo_ref, o_ref, oc_ref):
    for r0 in range(0, x_ref.shape[0], ROW_TILE):
        rows = slice(r0, r0 + ROW_TILE)
        merged = (g0_ref[rows, :].astype(F32) * jnp.dot(ya_ref[rows, :], wa_ref[...], preferred_element_type=F32)
                  + g1_ref[rows, :].astype(F32) * jnp.dot(yb_ref[rows, :], wb_ref[...], preferred_element_type=F32)
                  + g2_ref[rows, :].astype(F32) * jnp.dot(yc_ref[rows, :], wc_ref[...], preferred_element_type=F32))
        x = x_ref[rows, :] + jnp.dot(merged.astype(BF16), wo_ref[...], preferred_element_type=F32)

        hq = _rms_norm(x, gx_ref[...]).astype(BF16)
        q = (jnp.dot(hq, wq_ref[...], preferred_element_type=F32) * (XA_DIM ** -0.5)).astype(BF16)
        for h in range(XA_HEADS):
            c0 = h * XA_DIM
            kh = kv_ref[:, c0:c0 + XA_DIM]
            vh = kv_ref[:, D_MODEL + c0:D_MODEL + c0 + XA_DIM]
            s = lax.dot_general(q[:, c0:c0 + XA_DIM], kh, (((1,), (1,)), ((), ())), preferred_element_type=F32)
            p = jnp.exp(s - jnp.max(s, axis=-1, keepdims=True))
            l = jnp.sum(p, axis=-1, keepdims=True)
            o = jnp.dot(p.astype(BF16), vh, preferred_element_type=F32) / l
            oc_ref[rows, c0:c0 + XA_DIM] = o.astype(BF16)
        o_ref[rows, :] = x + jnp.dot(oc_ref[rows, :], wxo_ref[...], preferred_element_type=F32)


def _merge_xattn(x2d, batch, seq, ya, yb, yc, zmix, wa, wb, wc, wo, gx, wq, kv, wxo):
    nt = seq // MERGE_TILE
    n_mem = kv.shape[0] // batch
    tile = lambda width, col=0: pl.BlockSpec((MERGE_TILE, width), lambda b, t: (b * nt + t, col))
    return pl.pallas_call(
        _merge_xattn_kernel,
        grid=(batch, nt),
        in_specs=[
            tile(D_MODEL),
            tile(D_LRU),
            tile(D_CONV),
            tile(D_DA),
            tile(D_MODEL, 0),
            tile(D_MODEL, 1),
            tile(D_MODEL, 2),
            _resident((D_LRU, D_MODEL)),
            _resident((D_CONV, D_MODEL)),
            _resident((D_DA, D_MODEL)),
            _resident((D_MODEL, D_MODEL)),
            _resident((1, D_MODEL)),
            _resident((D_MODEL, D_MODEL)),
            pl.BlockSpec((n_mem, 2 * D_MODEL), lambda b, t: (b, 0)),
            _resident((D_MODEL, D_MODEL)),
        ],
        out_specs=tile(D_MODEL),
        out_shape=jax.ShapeDtypeStruct(x2d.shape, F32),
        scratch_shapes=[pltpu.VMEM((MERGE_TILE, D_MODEL), BF16)],
        compiler_params=_params(("parallel", "arbitrary")),
        name="merge_xattn",
    )(x2d, ya, yb, yc, zmix, zmix, zmix, wa, wb, wc, wo, gx, wq, kv, wxo)


def _ffn_kernel(x_ref, g_ref, w1_ref, w3_ref, cw_ref, cb_ref, w2_ref, fg_ref, o_ref,
                ae_ref, halo_ref, hm_ref, *, final_norm):
    halo = SUBLANES
    ts = SEQ_TILE

    @pl.when(pl.program_id(1) == 0)
    def _():
        halo_ref[...] = jnp.zeros(halo_ref.shape, F32)

    for r0 in range(0, x_ref.shape[0], ts):
        rows = slice(r0, r0 + ts)
        x = x_ref[rows, :]
        hf = _rms_norm(x, g_ref[...]).astype(BF16)
        for c0 in range(0, D_FF, FFN_COL_CHUNK):
            cs = slice(c0, c0 + FFN_COL_CHUNK)
            a = jnp.dot(hf, w1_ref[:, cs], preferred_element_type=F32)
            ae_ref[0:halo, :] = halo_ref[:, cs]
            ae_ref[halo:halo + ts, :] = a
            halo_ref[:, cs] = a[ts - halo:ts, :]
            y = cb_ref[:, cs] + cw_ref[FFN_CONV - 1:FFN_CONV, cs] * a
            for j in range(FFN_CONV - 1):
                off = halo - (FFN_CONV - 1) + j
                y = y + cw_ref[j:j + 1, cs] * ae_ref[off:off + ts, :]
            up = jnp.dot(hf, w3_ref[:, cs], preferred_element_type=F32)
            hm_ref[rows, cs] = (y * _sigmoid(y) * up).astype(BF16)
        out = x + jnp.dot(hm_ref[rows, :], w2_ref[...], preferred_element_type=F32)
        if final_norm:
            out = _rms_norm(out, fg_ref[...])
        o_ref[rows, :] = out


def _ffn(x2d, batch, seq, g, w1, w3, cw, cb, w2, fg, final_norm):
    nt = seq // FFN_TILE
    return pl.pallas_call(
        functools.partial(_ffn_kernel, final_norm=final_norm),
        grid=(batch, nt),
        in_specs=[
            pl.BlockSpec((FFN_TILE, D_MODEL), lambda b, t: (b * nt + t, 0)),
            _resident((1, D_MODEL)),
            _resident((D_MODEL, D_FF)),
            _resident((D_MODEL, D_FF)),
            _resident((FFN_CONV, D_FF)),
            _resident((1, D_FF)),
            _resident((D_FF, D_MODEL)),
            _resident((1, D_MODEL)),
        ],
        out_specs=pl.BlockSpec((FFN_TILE, D_MODEL), lambda b, t: (b * nt + t, 0)),
        out_shape=jax.ShapeDtypeStruct(x2d.shape, F32),
        scratch_shapes=[
            pltpu.VMEM((SEQ_TILE + SUBLANES, FFN_COL_CHUNK), F32),
            pltpu.VMEM((SUBLANES, D_FF), F32),
            pltpu.VMEM((FFN_TILE, D_FF), BF16),
        ],
        compiler_params=_params(("parallel", "arbitrary")),
        name="ffn_final" if final_norm else "ffn",
    )(x2d, g, w1, w3, cw, cb, w2, fg)


def _block_diag(w):
    nb, bi, bj = w.shape
    eye = jnp.eye(nb, dtype=w.dtype)
    return jnp.einsum('hij,hg->higj', w, eye).reshape(nb * bi, nb * bj)


def kernel(x, mem, rel_bias, norm_mix_g, w_in, w_gate, b_gate, lru_conv_w, lru_conv_b, lru_wr, lru_br, lru_wi, lru_bi, lru_lambda, lru_out, cm_conv_w, cm_conv_b, cm_ln_g, cm_ln_b, cm_out, da_lambda, da_subln_g, da_out, w_o, norm_xa_g, norm_mem_g, xa_wq, xa_wkv, xa_wo, norm_ffn_g, ffn_w1, ffn_w3, ffn_conv_w, ffn_conv_b, ffn_w2, final_g):
    batch, seq, d = x.shape
    n_mem = mem.shape[1]
    depth = w_in.shape[0]
    assert d == D_MODEL and seq % SEQ_TILE == 0 and seq % ROW_TILE == 0 and seq % DA_TQ == 0
    assert w_in.shape[2] == D_IN

    bias_tiles = _bias_tiles(rel_bias)
    x2d = x.reshape(batch * seq, d)
    mem2d = mem.reshape(batch * n_mem, d)
    row = lambda v: v.reshape(1, -1)
    no_bias = jnp.zeros((1, MIX_COL_CHUNK), F32)

    for l in range(depth):
        w_mix = jnp.concatenate([w_gate[l, 0], w_gate[l, 1], w_gate[l, 2], w_in[l, :, IN_Q:], w_in[l, :, :IN_Q]],
                                axis=1).astype(BF16)
        zmix = _norm_proj(x2d, row(norm_mix_g[l]), w_mix, row(b_gate[l]), "mix_proj",
                          n_sig=N_GATE, scaled=(COL_Q, COL_K, DA_Q_SCALE))
        wri = jnp.concatenate([_block_diag(lru_wr[l]), _block_diag(lru_wi[l])], axis=1).astype(BF16)
        bri = jnp.concatenate([lru_br[l], lru_bi[l]]).reshape(1, -1)
        ya = _lru_branch(zmix, batch, seq, lru_conv_w[l], row(lru_conv_b[l]), wri, bri, row(lru_lambda[l]))
        yb = _cconv_branch(zmix, batch, seq, cm_conv_w[l], row(cm_conv_b[l]), row(cm_ln_g[l]), row(cm_ln_b[l]))
        lam_init = 0.8 - 0.6 * math.exp(-0.3 * l)
        yc = _diff_attention(zmix, batch, seq, bias_tiles, da_lambda[l], row(da_subln_g[l]), lam_init)
        kv = _norm_proj(mem2d, row(norm_mem_g[l]), xa_wkv[l].astype(BF16), no_bias, "mem_kv")
        x2d = _merge_xattn(x2d, batch, seq, ya, yb, yc, zmix,
                           lru_out[l].astype(BF16), cm_out[l].astype(BF16), da_out[l].astype(BF16),
                           w_o[l].astype(BF16), row(norm_xa_g[l]), xa_wq[l].astype(BF16), kv,
                           xa_wo[l].astype(BF16))
        x2d = _ffn(x2d, batch, seq, row(norm_ffn_g[l]), ffn_w1[l].astype(BF16), ffn_w3[l].astype(BF16),
                   ffn_conv_w[l], row(ffn_conv_b[l]), ffn_w2[l].astype(BF16), row(final_g),
                   final_norm=(l == depth - 1))
    return x2d.reshape(batch, seq, d)
```

```python
import functools
import math

import jax
import jax.numpy as jnp
from jax import lax
from jax.experimental import pallas as pl
from jax.experimental.pallas import tpu as pltpu

F32 = jnp.float32
BF16 = jnp.bfloat16

D_MODEL = 1024
CHUNK = 64
D_LRU = 512
LRU_BLOCKS = 8
LRU_CONV = 4
LRU_C = 8.0
D_CONV = 512
CONV_WIDTH = 31
DA_HEADS = 4
DA_DIM = 64
DA_VDIM = 2 * DA_DIM
D_DA = DA_HEADS * DA_VDIM
REL_BUCKETS = 32
REL_MAX_DIST = 128
XA_HEADS = 4
XA_DIM = D_MODEL // XA_HEADS
D_FF = 2816
FFN_CONV = 3
N_BRANCH = 3
EPS = 1e-6

IN_LRU_X = 0
IN_LRU_G = IN_LRU_X + D_LRU
IN_CONV_A = IN_LRU_G + D_LRU
IN_CONV_G = IN_CONV_A + D_CONV
IN_Q = IN_CONV_G + D_CONV
IN_K = IN_Q + DA_HEADS * 2 * DA_DIM
IN_V = IN_K + DA_HEADS * 2 * DA_DIM
D_IN = IN_V + D_DA
N_GATE = N_BRANCH * D_MODEL
COL_Q = N_GATE
COL_K = COL_Q + DA_HEADS * 2 * DA_DIM
COL_V = COL_K + DA_HEADS * 2 * DA_DIM
COL_LRU_X = COL_V + D_DA
COL_LRU_G = COL_LRU_X + D_LRU
COL_CONV = COL_LRU_G + D_LRU
N_MIX = COL_CONV + D_CONV

SUBLANES = 8
LANES = 128
VMEM_LIMIT = 56 * 1024 * 1024
ROW_TILE = 512
MIX_COL_CHUNK = 512
SEQ_TILE = 512
FFN_TILE = 2 * SEQ_TILE
MERGE_TILE = 2 * ROW_TILE
CONV_ROWS = 32
LRU_ROWS = 32
CONV_HALO = 32
CONV_FIRST_TAP = CONV_HALO - (CONV_WIDTH - 1)
DA_TQ = 256
DA_TK = 256
FFN_COL_CHUNK = 256
DA_ONES_ROWS = 16
MASK_VALUE = -1e30
LOG2E = math.log2(math.e)
DA_Q_SCALE = DA_DIM ** -0.5 * LOG2E


def _resident(shape):
    nd = len(shape)
    return pl.BlockSpec(shape, lambda *_: (0,) * nd, pipeline_mode=pl.Buffered(1))


def _params(semantics):
    return pltpu.CompilerParams(dimension_semantics=semantics, vmem_limit_bytes=VMEM_LIMIT)


def _rms_norm(x, g):
    return x * lax.rsqrt(jnp.mean(x * x, axis=-1, keepdims=True) + EPS) * g


def _sigmoid(x):
    return 1.0 / (1.0 + jnp.exp(-x))


def _gelu_tanh(x):
    c = math.sqrt(2.0 / math.pi)
    return 0.5 * x * (1.0 + jnp.tanh(c * (x + 0.044715 * (x * x * x))))


def _norm_proj_kernel(x_ref, g_ref, w_ref, b_ref, o_ref, *, plan):
    h = _rms_norm(x_ref[...], g_ref[...]).astype(BF16)

    def proj(c0):
        return jnp.dot(h, w_ref[:, c0:c0 + MIX_COL_CHUNK], preferred_element_type=F32)

    for kind, wc, oc in plan:
        y = proj(wc)
        if kind == "sigmoid":
            y = _sigmoid(y + b_ref[:, wc:wc + MIX_COL_CHUNK])
        elif kind == "qscale":
            y = y * DA_Q_SCALE
        elif kind == "gelu":
            y = _gelu_tanh(y)
        elif kind == "glu":
            y = y * _sigmoid(proj(wc + MIX_COL_CHUNK))
        o_ref[:, oc:oc + MIX_COL_CHUNK] = y.astype(o_ref.dtype)


def _norm_proj(x2d, g, w, b, name, plan):
    n, d = x2d.shape
    n_out = max(oc for _, _, oc in plan) + MIX_COL_CHUNK
    assert n % ROW_TILE == 0 and w.shape[1] % MIX_COL_CHUNK == 0
    return pl.pallas_call(
        functools.partial(_norm_proj_kernel, plan=plan),
        grid=(n // ROW_TILE,),
        in_specs=[
            pl.BlockSpec((ROW_TILE, d), lambda i: (i, 0)),
            _resident((1, d)),
            _resident((d, w.shape[1])),
            _resident((1, b.shape[1])),
        ],
        out_specs=pl.BlockSpec((ROW_TILE, n_out), lambda i: (i, 0)),
        out_shape=jax.ShapeDtypeStruct((n, n_out), BF16),
        compiler_params=_params(("parallel",)),
        name=name,
    )(x2d, g, w, b)


def _mix_plan():
    chunks = lambda lo, hi: range(lo, hi, MIX_COL_CHUNK)
    plan = [("glu", COL_CONV, COL_CONV)]
    plan += [("gelu", c, c) for c in chunks(COL_LRU_G, COL_CONV)]
    plan += [("sigmoid", c, c) for c in chunks(0, N_GATE)]
    plan += [("qscale", c, c) for c in chunks(COL_Q, COL_K)]
    plan += [("plain", c, c) for c in chunks(COL_K, COL_LRU_G)]
    return tuple(plan)


def _lru_rows(r0, carry, softplus, xa_ref, ri_ref, zg_ref, o_ref):
    rows = slice(r0, r0 + LRU_ROWS)
    xa = xa_ref[rows, :]
    r = _sigmoid(ri_ref[rows, 0:D_LRU])
    gi = _sigmoid(ri_ref[rows, D_LRU:2 * D_LRU])
    a = jnp.exp((-LRU_C) * r * softplus)
    u = jnp.sqrt(1.0 - a * a) * (gi * xa)
    groups = LRU_ROWS // SUBLANES
    a = a.reshape(groups, SUBLANES, D_LRU)
    u = u.reshape(groups, SUBLANES, D_LRU)
    row = lax.broadcasted_iota(jnp.int32, (groups, SUBLANES, D_LRU), 1)
    s = 1
    while s < SUBLANES:
        keep = row >= s
        u = jnp.where(keep, a * pltpu.roll(u, s, 1) + u, u)
        a = jnp.where(keep, a * pltpu.roll(a, s, 1), a)
        s *= 2
    hs = []
    for g in range(groups):
        h = a[g] * carry + u[g]
        hs.append(h)
        carry = jnp.broadcast_to(h[SUBLANES - 1:SUBLANES, :], (SUBLANES, D_LRU))
    gate = zg_ref[rows, :].astype(F32)
    o_ref[rows, :] = (jnp.concatenate(hs, axis=0) * gate).astype(o_ref.dtype)
    return carry


def _lru_kernel(zx_ref, zg_ref, cw_ref, cb_ref, wri_ref, bri_ref, lam_ref, o_ref,
                xe_ref, xa_ref, ri_ref, carry_ref, sp_ref):
    ts = zx_ref.shape[0]
    halo = SUBLANES

    @pl.when(pl.program_id(1) == 0)
    def _():
        xe_ref[0:halo, :] = jnp.zeros((halo, D_LRU), F32)
        carry_ref[...] = jnp.zeros((SUBLANES, D_LRU), F32)
        nlam = -lam_ref[...]
        sp_ref[...] = jnp.maximum(nlam, 0.0) + jnp.log(1.0 + jnp.exp(-jnp.abs(nlam)))

    xe_ref[halo:halo + ts, :] = zx_ref[...].astype(F32)
    xa = cb_ref[...] + cw_ref[LRU_CONV - 1:LRU_CONV, :] * xe_ref[halo:halo + ts, :]
    for j in range(LRU_CONV - 1):
        off = halo - (LRU_CONV - 1) + j
        xa = xa + cw_ref[j:j + 1, :] * xe_ref[off:off + ts, :]
    xe_ref[0:halo, :] = xe_ref[ts:ts + halo, :]
    xa_ref[...] = xa
    ri_ref[...] = jnp.dot(xa.astype(BF16), wri_ref[...], preferred_element_type=F32) + bri_ref[...]

    carry = carry_ref[...]
    for r0 in range(0, ts, LRU_ROWS):
        carry = _lru_rows(r0, carry, sp_ref[...], xa_ref, ri_ref, zg_ref, o_ref)
    carry_ref[...] = carry


def _lru_branch(zmix, batch, seq, cw, cb, wri, bri, lam):
    nt = seq // SEQ_TILE
    return pl.pallas_call(
        _lru_kernel,
        grid=(batch, nt),
        in_specs=[
            pl.BlockSpec((SEQ_TILE, D_LRU), lambda b, t: (b * nt + t, COL_LRU_X // D_LRU)),
            pl.BlockSpec((SEQ_TILE, D_LRU), lambda b, t: (b * nt + t, COL_LRU_G // D_LRU)),
            _resident((LRU_CONV, D_LRU)),
            _resident((1, D_LRU)),
            _resident((D_LRU, 2 * D_LRU)),
            _resident((1, 2 * D_LRU)),
            _resident((1, D_LRU)),
        ],
        out_specs=pl.BlockSpec((SEQ_TILE, D_LRU), lambda b, t: (b * nt + t, 0)),
        out_shape=jax.ShapeDtypeStruct((batch * seq, D_LRU), BF16),
        scratch_shapes=[
            pltpu.VMEM((SEQ_TILE + SUBLANES, D_LRU), F32),
            pltpu.VMEM((SEQ_TILE, D_LRU), F32),
            pltpu.VMEM((SEQ_TILE, 2 * D_LRU), F32),
            pltpu.VMEM((SUBLANES, D_LRU), F32),
            pltpu.VMEM((1, D_LRU), F32),
        ],
        compiler_params=_params(("parallel", "arbitrary")),
        name="lru_branch",
    )(zmix, zmix, cw, cb, wri, bri, lam)


def _cconv_rows(r0, b_ref, lg_ref, lb_ref, o_ref, ce_ref, sh_ref, wb_ref):
    groups = range(0, CONV_ROWS, SUBLANES)
    accs = [jnp.broadcast_to(b_ref[...], (SUBLANES, D_CONV)) for _ in groups]
    for j in range(CONV_WIDTH):
        r = (CONV_FIRST_TAP + j) % SUBLANES
        base = r0 + CONV_FIRST_TAP + j - r
        wb = wb_ref[j]
        for k, g0 in enumerate(groups):
            lo = base + g0
            src = ce_ref[lo:lo + SUBLANES, :] if r == 0 else sh_ref[r - 1, lo:lo + SUBLANES, :]
            accs[k] = accs[k] + wb * src
    acc = jnp.concatenate(accs, axis=0)
    mu = jnp.mean(acc, axis=-1, keepdims=True)
    xc = acc - mu
    y = xc * lax.rsqrt(jnp.mean(xc * xc, axis=-1, keepdims=True) + EPS) * lg_ref[...] + lb_ref[...]
    o_ref[r0:r0 + CONV_ROWS, :] = (y * _sigmoid(y)).astype(o_ref.dtype)


def _cconv_kernel(c_ref, w_ref, b_ref, lg_ref, lb_ref, o_ref, ce_ref, sh_ref, wb_ref):
    ts = c_ref.shape[0]

    @pl.when(pl.program_id(1) == 0)
    def _():
        ce_ref[0:CONV_HALO, :] = jnp.zeros((CONV_HALO, D_CONV), F32)
        for j in range(CONV_WIDTH):
            wb_ref[j] = jnp.broadcast_to(w_ref[j:j + 1, :], (SUBLANES, D_CONV))

    ce_ref[CONV_HALO:CONV_HALO + ts, :] = c_ref[...].astype(F32)
    for r in range(1, SUBLANES):
        sh_ref[r - 1] = ce_ref[r:r + sh_ref.shape[1], :]
    for r0 in range(0, ts, CONV_ROWS):
        _cconv_rows(r0, b_ref, lg_ref, lb_ref, o_ref, ce_ref, sh_ref, wb_ref)
    ce_ref[0:CONV_HALO, :] = ce_ref[ts:ts + CONV_HALO, :]


def _cconv_branch(zmix, batch, seq, w, b, lg, lb):
    nt = seq // SEQ_TILE
    return pl.pallas_call(
        _cconv_kernel,
        grid=(batch, nt),
        in_specs=[
            pl.BlockSpec((SEQ_TILE, D_CONV), lambda b_, t: (b_ * nt + t, COL_CONV // D_CONV)),
            _resident((CONV_WIDTH, D_CONV)),
            _resident((1, D_CONV)),
            _resident((1, D_CONV)),
            _resident((1, D_CONV)),
        ],
        out_specs=pl.BlockSpec((SEQ_TILE, D_CONV), lambda b_, t: (b_ * nt + t, 0)),
        out_shape=jax.ShapeDtypeStruct((batch * seq, D_CONV), BF16),
        scratch_shapes=[
            pltpu.VMEM((SEQ_TILE + CONV_HALO, D_CONV), F32),
            pltpu.VMEM((SUBLANES - 1, SEQ_TILE + CONV_HALO - SUBLANES, D_CONV), F32),
            pltpu.VMEM((CONV_WIDTH, SUBLANES, D_CONV), F32),
        ],
        compiler_params=_params(("parallel", "arbitrary")),
        name="cconv_branch",
    )(zmix, w, b, lg, lb)


def _t5_bucket(rel):
    nb = REL_BUCKETS // 2
    ret = jnp.where(rel > 0, nb, 0)
    n = jnp.abs(rel)
    max_exact = nb // 2
    large = max_exact + (jnp.log(jnp.maximum(n, 1).astype(F32) / max_exact)
                         / math.log(REL_MAX_DIST / max_exact) * (nb - max_exact)).astype(jnp.int32)
    large = jnp.minimum(large, nb - 1)
    return ret + jnp.where(n < max_exact, n, large)


def _bias_tiles_kernel(tbl_ref, bucket_ref, o_ref, *, far_bucket):
    h = pl.program_id(0)
    bucket = bucket_ref[...]
    acc = jnp.zeros(bucket.shape, F32)
    for bkt in range(REL_BUCKETS):
        acc = jnp.where(bucket == bkt, tbl_ref[bkt, h], acc)
    acc = (acc - tbl_ref[far_bucket, h]) * LOG2E
    o_ref[...] = jnp.where(bucket < 0, MASK_VALUE, acc)


def _bias_tiles(rel_bias):
    kpos = jnp.arange(DA_TK, dtype=jnp.int32)[:, None]
    qpos = jnp.arange(DA_TQ, dtype=jnp.int32)[None, :]
    prev = _t5_bucket(kpos - DA_TK - qpos)
    diag = jnp.where(kpos // CHUNK <= qpos // CHUNK, _t5_bucket(kpos - qpos), -1)
    buckets = jnp.stack([prev, diag])
    assert DA_TK >= REL_MAX_DIST and DA_TQ == DA_TK
    far_bucket = REL_BUCKETS // 2 - 1
    return pl.pallas_call(
        functools.partial(_bias_tiles_kernel, far_bucket=far_bucket),
        grid=(DA_HEADS,),
        in_specs=[
            pl.BlockSpec(memory_space=pltpu.SMEM),
            pl.BlockSpec((2, DA_TK, DA_TQ), lambda h: (0, 0, 0)),
        ],
        out_specs=pl.BlockSpec((None, 2, DA_TK, DA_TQ), lambda h: (h, 0, 0, 0)),
        out_shape=jax.ShapeDtypeStruct((DA_HEADS, 2, DA_TK, DA_TQ), F32),
        compiler_params=_params(("parallel",)),
        name="bias_tiles",
    )(rel_bias, buckets)


def _da_kernel(q_ref, k_ref, v_ref, bias_ref, lam_ref, sg_ref, o_ref,
               vt_ref, m_ref, l_ref, acc_ref, s_ref, *, lam_init):
    tq, tk = DA_TQ, DA_TK
    blk = 2 * DA_DIM
    chains = [(h, c) for h in range(DA_HEADS) for c in range(2)]

    ones = jnp.ones((DA_ONES_ROWS, 2 * tk), BF16)
    for h in range(DA_HEADS):
        for jp in range(vt_ref.shape[1]):
            for half in range(2):
                r0 = (2 * jp + half) * tk
                v = v_ref[r0:r0 + tk, h * DA_VDIM:(h + 1) * DA_VDIM]
                vt_ref[h, jp, 0:DA_VDIM, half * tk:(half + 1) * tk] = v.astype(F32).T.astype(BF16)
            vt_ref[h, jp, DA_VDIM:DA_VDIM + DA_ONES_ROWS, :] = ones

    lv = lam_ref[...]
    lam = (jnp.exp(jnp.sum(lv[0:1, :] * lv[1:2, :], axis=-1, keepdims=True))
           - jnp.exp(jnp.sum(lv[2:3, :] * lv[3:4, :], axis=-1, keepdims=True)) + lam_init)
    lane = lax.broadcasted_iota(jnp.int32, (tq, blk), 1)

    def query_tile(i, carry):
        _da_query_tile(i, lam, lane, chains, q_ref, k_ref, bias_ref, sg_ref, o_ref,
                       vt_ref, m_ref, l_ref, acc_ref, s_ref, lam_init)
        return carry

    lax.fori_loop(0, q_ref.shape[0] // tq, query_tile, 0)


def _da_query_tile(i, lam, lane, chains, q_ref, k_ref, bias_ref, sg_ref, o_ref,
                   vt_ref, m_ref, l_ref, acc_ref, s_ref, lam_init):
    tq, tk = DA_TQ, DA_TK
    blk = 2 * DA_DIM
    q0 = pl.multiple_of(i * tq, tq)
    qz = []
    for h, c in chains:
        q = q_ref[pl.ds(q0, tq), h * blk:(h + 1) * blk]
        qz.append(jnp.where((lane >= DA_DIM) == (c == 1), q, jnp.zeros_like(q)))

    m_ref[...] = jnp.full(m_ref.shape, MASK_VALUE, F32)
    l_ref[...] = jnp.zeros(l_ref.shape, F32)
    acc_ref[...] = jnp.zeros(acc_ref.shape, F32)

    def step(jp, nears):
        nkeys = len(nears) * tk
        k0 = pl.multiple_of(jp * (2 * tk), 2 * tk)

        def logits(n):
            h = chains[n][0]
            kb = k_ref[pl.ds(k0, nkeys), h * blk:(h + 1) * blk]
            s = lax.dot_general(kb, qz[n], (((1,), (1,)), ((), ())), preferred_element_type=F32)
            if all(near is None for near in nears):
                return s
            parts = [s[t * tk:(t + 1) * tk, :] if near is None else s[t * tk:(t + 1) * tk, :] + bias_ref[h, near]
                     for t, near in enumerate(nears)]
            return jnp.concatenate(parts, axis=0)

        block_max = []
        for n in range(len(chains)):
            s = logits(n)
            s_ref[n, 0:nkeys, :] = s
            block_max.append(jnp.max(s, axis=0, keepdims=True))
        alphas, probs = [], []
        for n in range(len(chains)):
            m_prev = m_ref[n]
            m_new = jnp.maximum(m_prev, block_max[n])
            alphas.append(jnp.exp2(m_prev - m_new))
            probs.append(jnp.exp2(s_ref[n, 0:nkeys, :] - m_new).astype(BF16))
            m_ref[n] = m_new
        for n, (h, c) in enumerate(chains):
            pv = jnp.dot(vt_ref[h, jp, :, 0:nkeys], probs[n], preferred_element_type=F32)
            acc_ref[n] = alphas[n] * acc_ref[n] + pv[0:DA_VDIM, :]
            l_ref[n] = alphas[n] * l_ref[n] + pv[DA_VDIM:DA_VDIM + 1, :]

    def far_body(jp, carry):
        step(jp, (None, None))
        return carry

    lax.fori_loop(0, jnp.maximum(i - 1, 0) // 2, far_body, 0)
    odd = (i % 2) == 1

    @pl.when(odd)
    def _():
        step((i - 1) // 2, (0, 1))

    @pl.when(jnp.logical_and(jnp.logical_not(odd), i >= 2))
    def _():
        step((i - 2) // 2, (None, 0))

    @pl.when(jnp.logical_not(odd))
    def _():
        step(i // 2, (1,))

    for h in range(DA_HEADS):
        d = (acc_ref[2 * h] * (1.0 / l_ref[2 * h])
             - lam * (acc_ref[2 * h + 1] * (1.0 / l_ref[2 * h + 1])))
        y = d * lax.rsqrt(jnp.mean(d * d, axis=0, keepdims=True) + EPS)
        o_ref[pl.ds(q0, tq), h * DA_VDIM:(h + 1) * DA_VDIM] = (
            y.T * sg_ref[...] * (1.0 - lam_init)).astype(o_ref.dtype)


def _diff_attention(zmix, batch, seq, bias_tiles, lam_vec, subln_g, lam_init):
    assert COL_Q % D_DA == 0 and COL_K % D_DA == 0 and COL_V % D_DA == 0
    assert seq % (2 * DA_TK) == 0 and seq % DA_TQ == 0
    return pl.pallas_call(
        functools.partial(_da_kernel, lam_init=lam_init),
        grid=(batch,),
        in_specs=[
            pl.BlockSpec((seq, D_DA), lambda b: (b, COL_Q // D_DA)),
            pl.BlockSpec((seq, D_DA), lambda b: (b, COL_K // D_DA)),
            pl.BlockSpec((seq, D_DA), lambda b: (b, COL_V // D_DA)),
            _resident((DA_HEADS, 2, DA_TK, DA_TQ)),
            _resident((4, DA_DIM)),
            _resident((1, DA_VDIM)),
        ],
        out_specs=pl.BlockSpec((seq, D_DA), lambda b: (b, 0)),
        out_shape=jax.ShapeDtypeStruct((batch * seq, D_DA), BF16),
        scratch_shapes=[
            pltpu.VMEM((DA_HEADS, seq // (2 * DA_TK), DA_VDIM + DA_ONES_ROWS, 2 * DA_TK), BF16),
            pltpu.VMEM((2 * DA_HEADS, 1, DA_TQ), F32),
            pltpu.VMEM((2 * DA_HEADS, 1, DA_TQ), F32),
            pltpu.VMEM((2 * DA_HEADS, DA_VDIM, DA_TQ), F32),
            pltpu.VMEM((2 * DA_HEADS, 2 * DA_TK, DA_TQ), F32),
        ],
        compiler_params=_params(("parallel",)),
        name="diff_attention",
    )(zmix, zmix, zmix, bias_tiles, lam_vec, subln_g)


def _merge_xattn_kernel(x_ref, ya_ref, yb_ref, yc_ref, g0_ref, g1_ref, g2_ref,
                        wa_ref, wb_ref, wc_ref, wo_ref, gx_ref, wq_ref, kv_ref, wxo_ref, o_ref, oc_ref):
    for r0 in range(0, x_ref.shape[0], ROW_TILE):
        rows = slice(r0, r0 + ROW_TILE)
        merged = (g0_ref[rows, :].astype(F32) * jnp.dot(ya_ref[rows, :], wa_ref[...], preferred_element_type=F32)
                  + g1_ref[rows, :].astype(F32) * jnp.dot(yb_ref[rows, :], wb_ref[...], preferred_element_type=F32)
                  + g2_ref[rows, :].astype(F32) * jnp.dot(yc_ref[rows, :], wc_ref[...], preferred_element_type=F32))
        x = x_ref[rows, :] + jnp.dot(merged.astype(BF16), wo_ref[...], preferred_element_type=F32)

        hq = _rms_norm(x, gx_ref[...]).astype(BF16)
        q = (jnp.dot(hq, wq_ref[...], preferred_element_type=F32) * (XA_DIM ** -0.5)).astype(BF16)
        for h in range(XA_HEADS):
            c0 = h * XA_DIM
            kh = kv_ref[:, c0:c0 + XA_DIM]
            vh = kv_ref[:, D_MODEL + c0:D_MODEL + c0 + XA_DIM]
            s = lax.dot_general(q[:, c0:c0 + XA_DIM], kh, (((1,), (1,)), ((), ())), preferred_element_type=F32)
            p = jnp.exp(s - jnp.max(s, axis=-1, keepdims=True))
            l = jnp.sum(p, axis=-1, keepdims=True)
            o = jnp.dot(p.astype(BF16), vh, preferred_element_type=F32) / l
            oc_ref[rows, c0:c0 + XA_DIM] = o.astype(BF16)
        o_ref[rows, :] = x + jnp.dot(oc_ref[rows, :], wxo_ref[...], preferred_element_type=F32)


def _merge_xattn(x2d, batch, seq, ya, yb, yc, zmix, wa, wb, wc, wo, gx, wq, kv, wxo):
    nt = seq // MERGE_TILE
    n_mem = kv.shape[0] // batch
    tile = lambda width, col=0: pl.BlockSpec((MERGE_TILE, width), lambda b, t: (b * nt + t, col))
    return pl.pallas_call(
        _merge_xattn_kernel,
        grid=(batch, nt),
        in_specs=[
            tile(D_MODEL),
            tile(D_LRU),
            tile(D_CONV),
            tile(D_DA),
            tile(D_MODEL, 0),
            tile(D_MODEL, 1),
            tile(D_MODEL, 2),
            _resident((D_LRU, D_MODEL)),
            _resident((D_CONV, D_MODEL)),
            _resident((D_DA, D_MODEL)),
            _resident((D_MODEL, D_MODEL)),
            _resident((1, D_MODEL)),
            _resident((D_MODEL, D_MODEL)),
            pl.BlockSpec((n_mem, 2 * D_MODEL), lambda b, t: (b, 0)),
            _resident((D_MODEL, D_MODEL)),
        ],
        out_specs=tile(D_MODEL),
        out_shape=jax.ShapeDtypeStruct(x2d.shape, F32),
        scratch_shapes=[pltpu.VMEM((MERGE_TILE, D_MODEL), BF16)],
        compiler_params=_params(("parallel", "arbitrary")),
        name="merge_xattn",
    )(x2d, ya, yb, yc, zmix, zmix, zmix, wa, wb, wc, wo, gx, wq, kv, wxo)


def _ffn_kernel(x_ref, g_ref, w1_ref, w3_ref, cw_ref, cb_ref, w2_ref, fg_ref, o_ref,
                ae_ref, halo_ref, hm_ref, *, final_norm):
    halo = SUBLANES
    ts = SEQ_TILE

    @pl.when(pl.program_id(1) == 0)
    def _():
        halo_ref[...] = jnp.zeros(halo_ref.shape, F32)

    for r0 in range(0, x_ref.shape[0], ts):
        rows = slice(r0, r0 + ts)
        x = x_ref[rows, :]
        hf = _rms_norm(x, g_ref[...]).astype(BF16)
        for c0 in range(0, D_FF, FFN_COL_CHUNK):
            cs = slice(c0, c0 + FFN_COL_CHUNK)
            a = jnp.dot(hf, w1_ref[:, cs], preferred_element_type=F32)
            ae_ref[0:halo, :] = halo_ref[:, cs]
            ae_ref[halo:halo + ts, :] = a
            halo_ref[:, cs] = a[ts - halo:ts, :]
            y = cb_ref[:, cs] + cw_ref[FFN_CONV - 1:FFN_CONV, cs] * a
            for j in range(FFN_CONV - 1):
                off = halo - (FFN_CONV - 1) + j
                y = y + cw_ref[j:j + 1, cs] * ae_ref[off:off + ts, :]
            up = jnp.dot(hf, w3_ref[:, cs], preferred_element_type=F32)
            hm_ref[rows, cs] = (y * _sigmoid(y) * up).astype(BF16)
        out = x + jnp.dot(hm_ref[rows, :], w2_ref[...], preferred_element_type=F32)
        if final_norm:
            out = _rms_norm(out, fg_ref[...])
        o_ref[rows, :] = out


def _ffn(x2d, batch, seq, g, w1, w3, cw, cb, w2, fg, final_norm):
    nt = seq // FFN_TILE
    return pl.pallas_call(
        functools.partial(_ffn_kernel, final_norm=final_norm),
        grid=(batch, nt),
        in_specs=[
            pl.BlockSpec((FFN_TILE, D_MODEL), lambda b, t: (b * nt + t, 0)),
            _resident((1, D_MODEL)),
            _resident((D_MODEL, D_FF)),
            _resident((D_MODEL, D_FF)),
            _resident((FFN_CONV, D_FF)),
            _resident((1, D_FF)),
            _resident((D_FF, D_MODEL)),
            _resident((1, D_MODEL)),
        ],
        out_specs=pl.BlockSpec((FFN_TILE, D_MODEL), lambda b, t: (b * nt + t, 0)),
        out_shape=jax.ShapeDtypeStruct(x2d.shape, F32),
        scratch_shapes=[
            pltpu.VMEM((SEQ_TILE + SUBLANES, FFN_COL_CHUNK), F32),
            pltpu.VMEM((SUBLANES, D_FF), F32),
            pltpu.VMEM((FFN_TILE, D_FF), BF16),
        ],
        compiler_params=_params(("parallel", "arbitrary")),
        name="ffn_final" if final_norm else "ffn",
    )(x2d, g, w1, w3, cw, cb, w2, fg)


def _block_diag(w):
    nb, bi, bj = w.shape
    eye = jnp.eye(nb, dtype=w.dtype)
    return jnp.einsum('hij,hg->higj', w, eye).reshape(nb * bi, nb * bj)


def kernel(x, mem, rel_bias, norm_mix_g, w_in, w_gate, b_gate, lru_conv_w, lru_conv_b, lru_wr, lru_br, lru_wi, lru_bi, lru_lambda, lru_out, cm_conv_w, cm_conv_b, cm_ln_g, cm_ln_b, cm_out, da_lambda, da_subln_g, da_out, w_o, norm_xa_g, norm_mem_g, xa_wq, xa_wkv, xa_wo, norm_ffn_g, ffn_w1, ffn_w3, ffn_conv_w, ffn_conv_b, ffn_w2, final_g):
    batch, seq, d = x.shape
    n_mem = mem.shape[1]
    depth = w_in.shape[0]
    assert d == D_MODEL and seq % SEQ_TILE == 0 and seq % ROW_TILE == 0 and seq % DA_TQ == 0
    assert w_in.shape[2] == D_IN

    bias_tiles = _bias_tiles(rel_bias)
    x2d = x.reshape(batch * seq, d)
    mem2d = mem.reshape(batch * n_mem, d)
    row = lambda v: v.reshape(1, -1)
    no_bias = jnp.zeros((1, MIX_COL_CHUNK), F32)
    kv_plan = tuple(("plain", c, c) for c in range(0, 2 * D_MODEL, MIX_COL_CHUNK))

    for l in range(depth):
        w_mix = jnp.concatenate([w_gate[l, 0], w_gate[l, 1], w_gate[l, 2], w_in[l, :, IN_Q:], w_in[l, :, :IN_Q]],
                                axis=1).astype(BF16)
        zmix = _norm_proj(x2d, row(norm_mix_g[l]), w_mix, row(b_gate[l]), "mix_proj", _mix_plan())
        wri = jnp.concatenate([_block_diag(lru_wr[l]), _block_diag(lru_wi[l])], axis=1).astype(BF16)
        bri = jnp.concatenate([lru_br[l], lru_bi[l]]).reshape(1, -1)
        ya = _lru_branch(zmix, batch, seq, lru_conv_w[l], row(lru_conv_b[l]), wri, bri, row(lru_lambda[l]))
        yb = _cconv_branch(zmix, batch, seq, cm_conv_w[l], row(cm_conv_b[l]), row(cm_ln_g[l]), row(cm_ln_b[l]))
        lam_init = 0.8 - 0.6 * math.exp(-0.3 * l)
        yc = _diff_attention(zmix, batch, seq, bias_tiles, da_lambda[l], row(da_subln_g[l]), lam_init)
        kv = _norm_proj(mem2d, row(norm_mem_g[l]), xa_wkv[l].astype(BF16), no_bias, "mem_kv", kv_plan)
        x2d = _merge_xattn(x2d, batch, seq, ya, yb, yc, zmix,
                           lru_out[l].astype(BF16), cm_out[l].astype(BF16), da_out[l].astype(BF16),
                           w_o[l].astype(BF16), row(norm_xa_g[l]), xa_wq[l].astype(BF16), kv,
                           xa_wo[l].astype(BF16))
        x2d = _ffn(x2d, batch, seq, row(norm_ffn_g[l]), ffn_w1[l].astype(BF16), ffn_w3[l].astype(BF16),
                   ffn_conv_w[l], row(ffn_conv_b[l]), ffn_w2[l].astype(BF16), row(final_g),
                   final_norm=(l == depth - 1))
    return x2d.reshape(batch, seq, d)
```

```python
import functools
import math

import jax
import jax.numpy as jnp
from jax import lax
from jax.experimental import pallas as pl
from jax.experimental.pallas import tpu as pltpu

F32 = jnp.float32
BF16 = jnp.bfloat16

D_MODEL = 1024
CHUNK = 64
D_LRU = 512
LRU_BLOCKS = 8
LRU_CONV = 4
LRU_C = 8.0
D_CONV = 512
CONV_WIDTH = 31
DA_HEADS = 4
DA_DIM = 64
DA_VDIM = 2 * DA_DIM
D_DA = DA_HEADS * DA_VDIM
REL_BUCKETS = 32
REL_MAX_DIST = 128
XA_HEADS = 4
XA_DIM = D_MODEL // XA_HEADS
D_FF = 2816
FFN_CONV = 3
N_BRANCH = 3
EPS = 1e-6

IN_LRU_X = 0
IN_LRU_G = IN_LRU_X + D_LRU
IN_CONV_A = IN_LRU_G + D_LRU
IN_CONV_G = IN_CONV_A + D_CONV
IN_Q = IN_CONV_G + D_CONV
IN_K = IN_Q + DA_HEADS * 2 * DA_DIM
IN_V = IN_K + DA_HEADS * 2 * DA_DIM
D_IN = IN_V + D_DA
N_GATE = N_BRANCH * D_MODEL
COL_Q = N_GATE
COL_K = COL_Q + DA_HEADS * 2 * DA_DIM
COL_V = COL_K + DA_HEADS * 2 * DA_DIM
COL_LRU_X = COL_V + D_DA
COL_LRU_G = COL_LRU_X + D_LRU
COL_CONV = COL_LRU_G + D_LRU
N_MIX = COL_CONV + D_CONV

SUBLANES = 8
LANES = 128
VMEM_LIMIT = 56 * 1024 * 1024
ROW_TILE = 512
MIX_COL_CHUNK = 512
SEQ_TILE = 512
FFN_TILE = 2 * SEQ_TILE
MERGE_TILE = 2 * ROW_TILE
MIX_TILE = 2 * ROW_TILE
CONV_ROWS = 32
LRU_ROWS = 32
CONV_HALO = 32
CONV_FIRST_TAP = CONV_HALO - (CONV_WIDTH - 1)
DA_TQ = 256
DA_TK = 256
FFN_COL_CHUNK = 256
DA_ONES_ROWS = 16
MASK_VALUE = -1e30
LOG2E = math.log2(math.e)
DA_Q_SCALE = DA_DIM ** -0.5 * LOG2E


def _resident(shape):
    nd = len(shape)
    return pl.BlockSpec(shape, lambda *_: (0,) * nd, pipeline_mode=pl.Buffered(1))


def _params(semantics):
    return pltpu.CompilerParams(dimension_semantics=semantics, vmem_limit_bytes=VMEM_LIMIT)


def _rms_norm(x, g):
    return x * lax.rsqrt(jnp.mean(x * x, axis=-1, keepdims=True) + EPS) * g


def _sigmoid(x):
    return 1.0 / (1.0 + jnp.exp(-x))


def _gelu_tanh(x):
    c = math.sqrt(2.0 / math.pi)
    return 0.5 * x * (1.0 + jnp.tanh(c * (x + 0.044715 * (x * x * x))))


def _norm_proj_kernel(x_ref, g_ref, b_ref, *refs, plan):
    w_refs, o_ref = refs[:-1], refs[-1]
    for r0 in range(0, x_ref.shape[0], ROW_TILE):
        rows = slice(r0, r0 + ROW_TILE)
        h = _rms_norm(x_ref[rows, :], g_ref[...]).astype(BF16)

        def proj(wi, c0, h=h):
            return jnp.dot(h, w_refs[wi][:, c0:c0 + MIX_COL_CHUNK], preferred_element_type=F32)

        for kind, wi, wc, oc in plan:
            y = proj(wi, wc)
            if kind == "sigmoid":
                y = _sigmoid(y + b_ref[:, oc:oc + MIX_COL_CHUNK])
            elif kind == "qscale":
                y = y * DA_Q_SCALE
            elif kind == "gelu":
                y = _gelu_tanh(y)
            elif kind == "glu":
                y = y * _sigmoid(proj(wi, wc + MIX_COL_CHUNK))
            o_ref[rows, oc:oc + MIX_COL_CHUNK] = y.astype(o_ref.dtype)


def _norm_proj(x2d, g, weights, b, name, plan, tile=ROW_TILE):
    n, d = x2d.shape
    n_out = max(oc for _, _, _, oc in plan) + MIX_COL_CHUNK
    assert n % tile == 0 and tile % ROW_TILE == 0
    return pl.pallas_call(
        functools.partial(_norm_proj_kernel, plan=plan),
        grid=(n // tile,),
        in_specs=[
            pl.BlockSpec((tile, d), lambda i: (i, 0)),
            _resident((1, d)),
            _resident((1, b.shape[1])),
        ] + [_resident(w.shape) for w in weights],
        out_specs=pl.BlockSpec((tile, n_out), lambda i: (i, 0)),
        out_shape=jax.ShapeDtypeStruct((n, n_out), BF16),
        compiler_params=_params(("parallel",)),
        name=name,
    )(x2d, g, b, *weights)


def _mix_plan():
    w_in = N_BRANCH
    plan = [("glu", w_in, IN_CONV_A, COL_CONV), ("gelu", w_in, IN_LRU_G, COL_LRU_G)]
    plan += [("sigmoid", j, c, j * D_MODEL + c) for j in range(N_BRANCH) for c in range(0, D_MODEL, MIX_COL_CHUNK)]
    plan += [("qscale", w_in, IN_Q, COL_Q), ("plain", w_in, IN_K, COL_K), ("plain", w_in, IN_V, COL_V),
             ("plain", w_in, IN_LRU_X, COL_LRU_X)]
    return tuple(plan)


def _lru_rows(r0, carry, softplus, xa_ref, ri_ref, zg_ref, o_ref):
    rows = slice(r0, r0 + LRU_ROWS)
    xa = xa_ref[rows, :]
    r = _sigmoid(ri_ref[rows, 0:D_LRU])
    gi = _sigmoid(ri_ref[rows, D_LRU:2 * D_LRU])
    a = jnp.exp((-LRU_C) * r * softplus)
    u = jnp.sqrt(1.0 - a * a) * (gi * xa)
    groups = LRU_ROWS // SUBLANES
    a = a.reshape(groups, SUBLANES, D_LRU)
    u = u.reshape(groups, SUBLANES, D_LRU)
    row = lax.broadcasted_iota(jnp.int32, (groups, SUBLANES, D_LRU), 1)
    s = 1
    while s < SUBLANES:
        keep = row >= s
        u = jnp.where(keep, a * pltpu.roll(u, s, 1) + u, u)
        a = jnp.where(keep, a * pltpu.roll(a, s, 1), a)
        s *= 2
    hs = []
    for g in range(groups):
        h = a[g] * carry + u[g]
        hs.append(h)
        carry = jnp.broadcast_to(h[SUBLANES - 1:SUBLANES, :], (SUBLANES, D_LRU))
    gate = zg_ref[rows, :].astype(F32)
    o_ref[rows, :] = (jnp.concatenate(hs, axis=0) * gate).astype(o_ref.dtype)
    return carry


def _lru_kernel(zx_ref, zg_ref, cw_ref, cb_ref, wri_ref, bri_ref, lam_ref, o_ref,
                xe_ref, xa_ref, ri_ref, carry_ref, sp_ref):
    ts = zx_ref.shape[0]
    halo = SUBLANES

    @pl.when(pl.program_id(1) == 0)
    def _():
        xe_ref[0:halo, :] = jnp.zeros((halo, D_LRU), F32)
        carry_ref[...] = jnp.zeros((SUBLANES, D_LRU), F32)
        nlam = -lam_ref[...]
        sp_ref[...] = jnp.maximum(nlam, 0.0) + jnp.log(1.0 + jnp.exp(-jnp.abs(nlam)))

    xe_ref[halo:halo + ts, :] = zx_ref[...].astype(F32)
    xa = cb_ref[...] + cw_ref[LRU_CONV - 1:LRU_CONV, :] * xe_ref[halo:halo + ts, :]
    for j in range(LRU_CONV - 1):
        off = halo - (LRU_CONV - 1) + j
        xa = xa + cw_ref[j:j + 1, :] * xe_ref[off:off + ts, :]
    xe_ref[0:halo, :] = xe_ref[ts:ts + halo, :]
    xa_ref[...] = xa
    ri_ref[...] = jnp.dot(xa.astype(BF16), wri_ref[...], preferred_element_type=F32) + bri_ref[...]

    carry = carry_ref[...]
    for r0 in range(0, ts, LRU_ROWS):
        carry = _lru_rows(r0, carry, sp_ref[...], xa_ref, ri_ref, zg_ref, o_ref)
    carry_ref[...] = carry


def _lru_branch(zmix, batch, seq, cw, cb, wri, bri, lam):
    nt = seq // SEQ_TILE
    return pl.pallas_call(
        _lru_kernel,
        grid=(batch, nt),
        in_specs=[
            pl.BlockSpec((SEQ_TILE, D_LRU), lambda b, t: (b * nt + t, COL_LRU_X // D_LRU)),
            pl.BlockSpec((SEQ_TILE, D_LRU), lambda b, t: (b * nt + t, COL_LRU_G // D_LRU)),
            _resident((LRU_CONV, D_LRU)),
            _resident((1, D_LRU)),
            _resident((D_LRU, 2 * D_LRU)),
            _resident((1, 2 * D_LRU)),
            _resident((1, D_LRU)),
        ],
        out_specs=pl.BlockSpec((SEQ_TILE, D_LRU), lambda b, t: (b * nt + t, 0)),
        out_shape=jax.ShapeDtypeStruct((batch * seq, D_LRU), BF16),
        scratch_shapes=[
            pltpu.VMEM((SEQ_TILE + SUBLANES, D_LRU), F32),
            pltpu.VMEM((SEQ_TILE, D_LRU), F32),
            pltpu.VMEM((SEQ_TILE, 2 * D_LRU), F32),
            pltpu.VMEM((SUBLANES, D_LRU), F32),
            pltpu.VMEM((1, D_LRU), F32),
        ],
        compiler_params=_params(("parallel", "arbitrary")),
        name="lru_branch",
    )(zmix, zmix, cw, cb, wri, bri, lam)


def _cconv_rows(r0, b_ref, lg_ref, lb_ref, o_ref, ce_ref, sh_ref, wb_ref):
    groups = range(0, CONV_ROWS, SUBLANES)
    accs = [jnp.broadcast_to(b_ref[...], (SUBLANES, D_CONV)) for _ in groups]
    for j in range(CONV_WIDTH):
        r = (CONV_FIRST_TAP + j) % SUBLANES
        base = r0 + CONV_FIRST_TAP + j - r
        wb = wb_ref[j]
        for k, g0 in enumerate(groups):
            lo = base + g0
            src = ce_ref[lo:lo + SUBLANES, :] if r == 0 else sh_ref[r - 1, lo:lo + SUBLANES, :]
            accs[k] = accs[k] + wb * src
    acc = jnp.concatenate(accs, axis=0)
    mu = jnp.mean(acc, axis=-1, keepdims=True)
    xc = acc - mu
    y = xc * lax.rsqrt(jnp.mean(xc * xc, axis=-1, keepdims=True) + EPS) * lg_ref[...] + lb_ref[...]
    o_ref[r0:r0 + CONV_ROWS, :] = (y * _sigmoid(y)).astype(o_ref.dtype)


def _cconv_kernel(c_ref, w_ref, b_ref, lg_ref, lb_ref, o_ref, ce_ref, sh_ref, wb_ref):
    ts = c_ref.shape[0]

    @pl.when(pl.program_id(1) == 0)
    def _():
        ce_ref[0:CONV_HALO, :] = jnp.zeros((CONV_HALO, D_CONV), F32)
        for j in range(CONV_WIDTH):
            wb_ref[j] = jnp.broadcast_to(w_ref[j:j + 1, :], (SUBLANES, D_CONV))

    ce_ref[CONV_HALO:CONV_HALO + ts, :] = c_ref[...].astype(F32)
    for r in range(1, SUBLANES):
        sh_ref[r - 1] = ce_ref[r:r + sh_ref.shape[1], :]
    for r0 in range(0, ts, CONV_ROWS):
        _cconv_rows(r0, b_ref, lg_ref, lb_ref, o_ref, ce_ref, sh_ref, wb_ref)
    ce_ref[0:CONV_HALO, :] = ce_ref[ts:ts + CONV_HALO, :]


def _cconv_branch(zmix, batch, seq, w, b, lg, lb):
    nt = seq // SEQ_TILE
    return pl.pallas_call(
        _cconv_kernel,
        grid=(batch, nt),
        in_specs=[
            pl.BlockSpec((SEQ_TILE, D_CONV), lambda b_, t: (b_ * nt + t, COL_CONV // D_CONV)),
            _resident((CONV_WIDTH, D_CONV)),
            _resident((1, D_CONV)),
            _resident((1, D_CONV)),
            _resident((1, D_CONV)),
        ],
        out_specs=pl.BlockSpec((SEQ_TILE, D_CONV), lambda b_, t: (b_ * nt + t, 0)),
        out_shape=jax.ShapeDtypeStruct((batch * seq, D_CONV), BF16),
        scratch_shapes=[
            pltpu.VMEM((SEQ_TILE + CONV_HALO, D_CONV), F32),
            pltpu.VMEM((SUBLANES - 1, SEQ_TILE + CONV_HALO - SUBLANES, D_CONV), F32),
            pltpu.VMEM((CONV_WIDTH, SUBLANES, D_CONV), F32),
        ],
        compiler_params=_params(("parallel", "arbitrary")),
        name="cconv_branch",
    )(zmix, w, b, lg, lb)


def _t5_bucket(rel):
    nb = REL_BUCKETS // 2
    ret = jnp.where(rel > 0, nb, 0)
    n = jnp.abs(rel)
    max_exact = nb // 2
    large = max_exact + (jnp.log(jnp.maximum(n, 1).astype(F32) / max_exact)
                         / math.log(REL_MAX_DIST / max_exact) * (nb - max_exact)).astype(jnp.int32)
    large = jnp.minimum(large, nb - 1)
    return ret + jnp.where(n < max_exact, n, large)


def _bias_tiles_kernel(tbl_ref, bucket_ref, o_ref, *, far_bucket):
    h = pl.program_id(0)
    bucket = bucket_ref[...]
    acc = jnp.zeros(bucket.shape, F32)
    for bkt in range(REL_BUCKETS):
        acc = jnp.where(bucket == bkt, tbl_ref[bkt, h], acc)
    acc = (acc - tbl_ref[far_bucket, h]) * LOG2E
    o_ref[...] = jnp.where(bucket < 0, MASK_VALUE, acc)


def _bias_tiles(rel_bias):
    kpos = jnp.arange(DA_TK, dtype=jnp.int32)[:, None]
    qpos = jnp.arange(DA_TQ, dtype=jnp.int32)[None, :]
    prev = _t5_bucket(kpos - DA_TK - qpos)
    diag = jnp.where(kpos // CHUNK <= qpos // CHUNK, _t5_bucket(kpos - qpos), -1)
    buckets = jnp.stack([prev, diag])
    assert DA_TK >= REL_MAX_DIST and DA_TQ == DA_TK
    far_bucket = REL_BUCKETS // 2 - 1
    return pl.pallas_call(
        functools.partial(_bias_tiles_kernel, far_bucket=far_bucket),
        grid=(DA_HEADS,),
        in_specs=[
            pl.BlockSpec(memory_space=pltpu.SMEM),
            pl.BlockSpec((2, DA_TK, DA_TQ), lambda h: (0, 0, 0)),
        ],
        out_specs=pl.BlockSpec((None, 2, DA_TK, DA_TQ), lambda h: (h, 0, 0, 0)),
        out_shape=jax.ShapeDtypeStruct((DA_HEADS, 2, DA_TK, DA_TQ), F32),
        compiler_params=_params(("parallel",)),
        name="bias_tiles",
    )(rel_bias, buckets)


def _da_kernel(q_ref, k_ref, v_ref, bias_ref, lam_ref, sg_ref, o_ref,
               vt_ref, m_ref, l_ref, acc_ref, s_ref, *, lam_init):
    tq, tk = DA_TQ, DA_TK
    blk = 2 * DA_DIM
    chains = [(h, c) for h in range(DA_HEADS) for c in range(2)]

    ones = jnp.ones((DA_ONES_ROWS, 2 * tk), BF16)
    for h in range(DA_HEADS):
        for jp in range(vt_ref.shape[1]):
            for half in range(2):
                r0 = (2 * jp + half) * tk
                v = v_ref[r0:r0 + tk, h * DA_VDIM:(h + 1) * DA_VDIM]
                vt_ref[h, jp, 0:DA_VDIM, half * tk:(half + 1) * tk] = v.astype(F32).T.astype(BF16)
            vt_ref[h, jp, DA_VDIM:DA_VDIM + DA_ONES_ROWS, :] = ones

    lv = lam_ref[...]
    lam = (jnp.exp(jnp.sum(lv[0:1, :] * lv[1:2, :], axis=-1, keepdims=True))
           - jnp.exp(jnp.sum(lv[2:3, :] * lv[3:4, :], axis=-1, keepdims=True)) + lam_init)
    lane = lax.broadcasted_iota(jnp.int32, (tq, blk), 1)

    def query_tile(i, carry):
        _da_query_tile(i, lam, lane, chains, q_ref, k_ref, bias_ref, sg_ref, o_ref,
                       vt_ref, m_ref, l_ref, acc_ref, s_ref, lam_init)
        return carry

    lax.fori_loop(0, q_ref.shape[0] // tq, query_tile, 0)


def _da_query_tile(i, lam, lane, chains, q_ref, k_ref, bias_ref, sg_ref, o_ref,
                   vt_ref, m_ref, l_ref, acc_ref, s_ref, lam_init):
    tq, tk = DA_TQ, DA_TK
    blk = 2 * DA_DIM
    q0 = pl.multiple_of(i * tq, tq)
    qz = []
    for h, c in chains:
        q = q_ref[pl.ds(q0, tq), h * blk:(h + 1) * blk]
        qz.append(jnp.where((lane >= DA_DIM) == (c == 1), q, jnp.zeros_like(q)))

    m_ref[...] = jnp.full(m_ref.shape, MASK_VALUE, F32)
    l_ref[...] = jnp.zeros(l_ref.shape, F32)
    acc_ref[...] = jnp.zeros(acc_ref.shape, F32)

    def step(jp, nears):
        nkeys = len(nears) * tk
        k0 = pl.multiple_of(jp * (2 * tk), 2 * tk)

        def logits(n):
            h = chains[n][0]
            kb = k_ref[pl.ds(k0, nkeys), h * blk:(h + 1) * blk]
            s = lax.dot_general(kb, qz[n], (((1,), (1,)), ((), ())), preferred_element_type=F32)
            if all(near is None for near in nears):
                return s
            parts = [s[t * tk:(t + 1) * tk, :] if near is None else s[t * tk:(t + 1) * tk, :] + bias_ref[h, near]
                     for t, near in enumerate(nears)]
            return jnp.concatenate(parts, axis=0)

        block_max = []
        for n in range(len(chains)):
            s = logits(n)
            s_ref[n, 0:nkeys, :] = s
            block_max.append(jnp.max(s, axis=0, keepdims=True))
        alphas, probs = [], []
        for n in range(len(chains)):
            m_prev = m_ref[n]
            m_new = jnp.maximum(m_prev, block_max[n])
            alphas.append(jnp.exp2(m_prev - m_new))
            probs.append(jnp.exp2(s_ref[n, 0:nkeys, :] - m_new).astype(BF16))
            m_ref[n] = m_new
        for n, (h, c) in enumerate(chains):
            pv = jnp.dot(vt_ref[h, jp, :, 0:nkeys], probs[n], preferred_element_type=F32)
            acc_ref[n] = alphas[n] * acc_ref[n] + pv[0:DA_VDIM, :]
            l_ref[n] = alphas[n] * l_ref[n] + pv[DA_VDIM:DA_VDIM + 1, :]

    def far_body(jp, carry):
        step(jp, (None, None))
        return carry

    lax.fori_loop(0, jnp.maximum(i - 1, 0) // 2, far_body, 0)
    odd = (i % 2) == 1

    @pl.when(odd)
    def _():
        step((i - 1) // 2, (0, 1))

    @pl.when(jnp.logical_and(jnp.logical_not(odd), i >= 2))
    def _():
        step((i - 2) // 2, (None, 0))

    @pl.when(jnp.logical_not(odd))
    def _():
        step(i // 2, (1,))

    for h in range(DA_HEADS):
        d = (acc_ref[2 * h] * (1.0 / l_ref[2 * h])
             - lam * (acc_ref[2 * h + 1] * (1.0 / l_ref[2 * h + 1])))
        y = d * lax.rsqrt(jnp.mean(d * d, axis=0, keepdims=True) + EPS)
        o_ref[pl.ds(q0, tq), h * DA_VDIM:(h + 1) * DA_VDIM] = (
            y.T * sg_ref[...] * (1.0 - lam_init)).astype(o_ref.dtype)


def _diff_attention(zmix, batch, seq, bias_tiles, lam_vec, subln_g, lam_init):
    assert COL_Q % D_DA == 0 and COL_K % D_DA == 0 and COL_V % D_DA == 0
    assert seq % (2 * DA_TK) == 0 and seq % DA_TQ == 0
    return pl.pallas_call(
        functools.partial(_da_kernel, lam_init=lam_init),
        grid=(batch,),
        in_specs=[
            pl.BlockSpec((seq, D_DA), lambda b: (b, COL_Q // D_DA)),
            pl.BlockSpec((seq, D_DA), lambda b: (b, COL_K // D_DA)),
            pl.BlockSpec((seq, D_DA), lambda b: (b, COL_V // D_DA)),
            _resident((DA_HEADS, 2, DA_TK, DA_TQ)),
            _resident((4, DA_DIM)),
            _resident((1, DA_VDIM)),
        ],
        out_specs=pl.BlockSpec((seq, D_DA), lambda b: (b, 0)),
        out_shape=jax.ShapeDtypeStruct((batch * seq, D_DA), BF16),
        scratch_shapes=[
            pltpu.VMEM((DA_HEADS, seq // (2 * DA_TK), DA_VDIM + DA_ONES_ROWS, 2 * DA_TK), BF16),
            pltpu.VMEM((2 * DA_HEADS, 1, DA_TQ), F32),
            pltpu.VMEM((2 * DA_HEADS, 1, DA_TQ), F32),
            pltpu.VMEM((2 * DA_HEADS, DA_VDIM, DA_TQ), F32),
            pltpu.VMEM((2 * DA_HEADS, 2 * DA_TK, DA_TQ), F32),
        ],
        compiler_params=_params(("parallel",)),
        name="diff_attention",
    )(zmix, zmix, zmix, bias_tiles, lam_vec, subln_g)


def _merge_xattn_kernel(x_ref, ya_ref, yb_ref, yc_ref, g0_ref, g1_ref, g2_ref,
                        wa_ref, wb_ref, wc_ref, wo_ref, gx_ref, wq_ref, kv_ref, wxo_ref, o_ref, oc_ref):
    for r0 in range(0, x_ref.shape[0], ROW_TILE):
        rows = slice(r0, r0 + ROW_TILE)
        merged = (g0_ref[rows, :].astype(F32) * jnp.dot(ya_ref[rows, :], wa_ref[...], preferred_element_type=F32)
                  + g1_ref[rows, :].astype(F32) * jnp.dot(yb_ref[rows, :], wb_ref[...], preferred_element_type=F32)
                  + g2_ref[rows, :].astype(F32) * jnp.dot(yc_ref[rows, :], wc_ref[...], preferred_element_type=F32))
        x = x_ref[rows, :] + jnp.dot(merged.astype(BF16), wo_ref[...], preferred_element_type=F32)

        hq = _rms_norm(x, gx_ref[...]).astype(BF16)
        q = (jnp.dot(hq, wq_ref[...], preferred_element_type=F32) * (XA_DIM ** -0.5)).astype(BF16)
        for h in range(XA_HEADS):
            c0 = h * XA_DIM
            kh = kv_ref[:, c0:c0 + XA_DIM]
            vh = kv_ref[:, D_MODEL + c0:D_MODEL + c0 + XA_DIM]
            s = lax.dot_general(q[:, c0:c0 + XA_DIM], kh, (((1,), (1,)), ((), ())), preferred_element_type=F32)
            p = jnp.exp(s - jnp.max(s, axis=-1, keepdims=True))
            l = jnp.sum(p, axis=-1, keepdims=True)
            o = jnp.dot(p.astype(BF16), vh, preferred_element_type=F32) / l
            oc_ref[rows, c0:c0 + XA_DIM] = o.astype(BF16)
        o_ref[rows, :] = x + jnp.dot(oc_ref[rows, :], wxo_ref[...], preferred_element_type=F32)


def _merge_xattn(x2d, batch, seq, ya, yb, yc, zmix, wa, wb, wc, wo, gx, wq, kv, wxo):
    nt = seq // MERGE_TILE
    n_mem = kv.shape[0] // batch
    tile = lambda width, col=0: pl.BlockSpec((MERGE_TILE, width), lambda b, t: (b * nt + t, col))
    return pl.pallas_call(
        _merge_xattn_kernel,
        grid=(batch, nt),
        in_specs=[
            tile(D_MODEL),
            tile(D_LRU),
            tile(D_CONV),
            tile(D_DA),
            tile(D_MODEL, 0),
            tile(D_MODEL, 1),
            tile(D_MODEL, 2),
            _resident((D_LRU, D_MODEL)),
            _resident((D_CONV, D_MODEL)),
            _resident((D_DA, D_MODEL)),
            _resident((D_MODEL, D_MODEL)),
            _resident((1, D_MODEL)),
            _resident((D_MODEL, D_MODEL)),
            pl.BlockSpec((n_mem, 2 * D_MODEL), lambda b, t: (b, 0)),
            _resident((D_MODEL, D_MODEL)),
        ],
        out_specs=tile(D_MODEL),
        out_shape=jax.ShapeDtypeStruct(x2d.shape, F32),
        scratch_shapes=[pltpu.VMEM((MERGE_TILE, D_MODEL), BF16)],
        compiler_params=_params(("parallel", "arbitrary")),
        name="merge_xattn",
    )(x2d, ya, yb, yc, zmix, zmix, zmix, wa, wb, wc, wo, gx, wq, kv, wxo)


def _ffn_kernel(x_ref, g_ref, w1_ref, w3_ref, cw_ref, cb_ref, w2_ref, fg_ref, o_ref,
                ae_ref, halo_ref, hm_ref, *, final_norm):
    halo = SUBLANES
    ts = SEQ_TILE

    @pl.when(pl.program_id(1) == 0)
    def _():
        halo_ref[...] = jnp.zeros(halo_ref.shape, F32)

    for r0 in range(0, x_ref.shape[0], ts):
        rows = slice(r0, r0 + ts)
        x = x_ref[rows, :]
        hf = _rms_norm(x, g_ref[...]).astype(BF16)
        for c0 in range(0, D_FF, FFN_COL_CHUNK):
            cs = slice(c0, c0 + FFN_COL_CHUNK)
            a = jnp.dot(hf, w1_ref[:, cs], preferred_element_type=F32)
            ae_ref[0:halo, :] = halo_ref[:, cs]
            ae_ref[halo:halo + ts, :] = a
            halo_ref[:, cs] = a[ts - halo:ts, :]
            y = cb_ref[:, cs] + cw_ref[FFN_CONV - 1:FFN_CONV, cs] * a
            for j in range(FFN_CONV - 1):
                off = halo - (FFN_CONV - 1) + j
                y = y + cw_ref[j:j + 1, cs] * ae_ref[off:off + ts, :]
            up = jnp.dot(hf, w3_ref[:, cs], preferred_element_type=F32)
            hm_ref[rows, cs] = (y * _sigmoid(y) * up).astype(BF16)
        out = x + jnp.dot(hm_ref[rows, :], w2_ref[...], preferred_element_type=F32)
        if final_norm:
            out = _rms_norm(out, fg_ref[...])
        o_ref[rows, :] = out


def _ffn(x2d, batch, seq, g, w1, w3, cw, cb, w2, fg, final_norm):
    nt = seq // FFN_TILE
    return pl.pallas_call(
        functools.partial(_ffn_kernel, final_norm=final_norm),
        grid=(batch, nt),
        in_specs=[
            pl.BlockSpec((FFN_TILE, D_MODEL), lambda b, t: (b * nt + t, 0)),
            _resident((1, D_MODEL)),
            _resident((D_MODEL, D_FF)),
            _resident((D_MODEL, D_FF)),
            _resident((FFN_CONV, D_FF)),
            _resident((1, D_FF)),
            _resident((D_FF, D_MODEL)),
            _resident((1, D_MODEL)),
        ],
        out_specs=pl.BlockSpec((FFN_TILE, D_MODEL), lambda b, t: (b * nt + t, 0)),
        out_shape=jax.ShapeDtypeStruct(x2d.shape, F32),
        scratch_shapes=[
            pltpu.VMEM((SEQ_TILE + SUBLANES, FFN_COL_CHUNK), F32),
            pltpu.VMEM((SUBLANES, D_FF), F32),
            pltpu.VMEM((FFN_TILE, D_FF), BF16),
        ],
        compiler_params=_params(("parallel", "arbitrary")),
        name="ffn_final" if final_norm else "ffn",
    )(x2d, g, w1, w3, cw, cb, w2, fg)


def _block_diag(w):
    nb, bi, bj = w.shape
    eye = jnp.eye(nb, dtype=w.dtype)
    return jnp.einsum('hij,hg->higj', w, eye).reshape(nb * bi, nb * bj)


def kernel(x, mem, rel_bias, norm_mix_g, w_in, w_gate, b_gate, lru_conv_w, lru_conv_b, lru_wr, lru_br, lru_wi, lru_bi, lru_lambda, lru_out, cm_conv_w, cm_conv_b, cm_ln_g, cm_ln_b, cm_out, da_lambda, da_subln_g, da_out, w_o, norm_xa_g, norm_mem_g, xa_wq, xa_wkv, xa_wo, norm_ffn_g, ffn_w1, ffn_w3, ffn_conv_w, ffn_conv_b, ffn_w2, final_g):
    batch, seq, d = x.shape
    n_mem = mem.shape[1]
    depth = w_in.shape[0]
    assert d == D_MODEL and seq % SEQ_TILE == 0 and seq % ROW_TILE == 0 and seq % DA_TQ == 0
    assert w_in.shape[2] == D_IN

    bias_tiles = _bias_tiles(rel_bias)
    x2d = x.reshape(batch * seq, d)
    mem2d = mem.reshape(batch * n_mem, d)
    row = lambda v: v.reshape(1, -1)
    no_bias = jnp.zeros((1, MIX_COL_CHUNK), F32)
    kv_plan = tuple(("plain", 0, c, c) for c in range(0, 2 * D_MODEL, MIX_COL_CHUNK))

    for l in range(depth):
        w_mix = [w_gate[l, j].astype(BF16) for j in range(N_BRANCH)] + [w_in[l].astype(BF16)]
        zmix = _norm_proj(x2d, row(norm_mix_g[l]), w_mix, row(b_gate[l]), "mix_proj", _mix_plan(),
                          tile=MIX_TILE)
        wri = jnp.concatenate([_block_diag(lru_wr[l]), _block_diag(lru_wi[l])], axis=1).astype(BF16)
        bri = jnp.concatenate([lru_br[l], lru_bi[l]]).reshape(1, -1)
        ya = _lru_branch(zmix, batch, seq, lru_conv_w[l], row(lru_conv_b[l]), wri, bri, row(lru_lambda[l]))
        yb = _cconv_branch(zmix, batch, seq, cm_conv_w[l], row(cm_conv_b[l]), row(cm_ln_g[l]), row(cm_ln_b[l]))
        lam_init = 0.8 - 0.6 * math.exp(-0.3 * l)
        yc = _diff_attention(zmix, batch, seq, bias_tiles, da_lambda[l], row(da_subln_g[l]), lam_init)
        kv = _norm_proj(mem2d, row(norm_mem_g[l]), [xa_wkv[l].astype(BF16)], no_bias, "mem_kv", kv_plan)
        x2d = _merge_xattn(x2d, batch, seq, ya, yb, yc, zmix,
                           lru_out[l].astype(BF16), cm_out[l].astype(BF16), da_out[l].astype(BF16),
                           w_o[l].astype(BF16), row(norm_xa_g[l]), xa_wq[l].astype(BF16), kv,
                           xa_wo[l].astype(BF16))
        x2d = _ffn(x2d, batch, seq, row(norm_ffn_g[l]), ffn_w1[l].astype(BF16), ffn_w3[l].astype(BF16),
                   ffn_conv_w[l], row(ffn_conv_b[l]), ffn_w2[l].astype(BF16), row(final_g),
                   final_norm=(l == depth - 1))
    return x2d.reshape(batch, seq, d)
```

```python
import functools
import math

import jax
import jax.numpy as jnp
from jax import lax
from jax.experimental import pallas as pl
from jax.experimental.pallas import tpu as pltpu

F32 = jnp.float32
BF16 = jnp.bfloat16

D_MODEL = 1024
CHUNK = 64
D_LRU = 512
LRU_BLOCKS = 8
LRU_CONV = 4
LRU_C = 8.0
D_CONV = 512
CONV_WIDTH = 31
DA_HEADS = 4
DA_DIM = 64
DA_VDIM = 2 * DA_DIM
D_DA = DA_HEADS * DA_VDIM
REL_BUCKETS = 32
REL_MAX_DIST = 128
XA_HEADS = 4
XA_DIM = D_MODEL // XA_HEADS
D_FF = 2816
FFN_CONV = 3
N_BRANCH = 3
EPS = 1e-6

IN_LRU_X = 0
IN_LRU_G = IN_LRU_X + D_LRU
IN_CONV_A = IN_LRU_G + D_LRU
IN_CONV_G = IN_CONV_A + D_CONV
IN_Q = IN_CONV_G + D_CONV
IN_K = IN_Q + DA_HEADS * 2 * DA_DIM
IN_V = IN_K + DA_HEADS * 2 * DA_DIM
D_IN = IN_V + D_DA
N_GATE = N_BRANCH * D_MODEL
COL_Q = N_GATE
COL_K = COL_Q + DA_HEADS * 2 * DA_DIM
COL_V = COL_K + DA_HEADS * 2 * DA_DIM
COL_LRU_X = COL_V + D_DA
COL_LRU_G = COL_LRU_X + D_LRU
COL_CONV = COL_LRU_G + D_LRU
N_MIX = COL_CONV + D_CONV

SUBLANES = 8
LANES = 128
VMEM_LIMIT = 56 * 1024 * 1024
ROW_TILE = 512
MIX_COL_CHUNK = 512
SEQ_TILE = 512
FFN_TILE = 2 * SEQ_TILE
MERGE_TILE = 2 * ROW_TILE
MIX_TILE = 2 * ROW_TILE
CONV_ROWS = 32
LRU_ROWS = 32
CONV_HALO = 32
CONV_FIRST_TAP = CONV_HALO - (CONV_WIDTH - 1)
DA_TQ = 256
DA_TK = 256
FFN_COL_CHUNK = 256
DA_ONES_ROWS = 16
MASK_VALUE = -1e30
LOG2E = math.log2(math.e)
DA_Q_SCALE = DA_DIM ** -0.5 * LOG2E


def _resident(shape):
    nd = len(shape)
    return pl.BlockSpec(shape, lambda *_: (0,) * nd, pipeline_mode=pl.Buffered(1))


def _params(semantics):
    return pltpu.CompilerParams(dimension_semantics=semantics, vmem_limit_bytes=VMEM_LIMIT)


def _rms_norm(x, g):
    return x * lax.rsqrt(jnp.mean(x * x, axis=-1, keepdims=True) + EPS) * g


def _sigmoid(x):
    return 1.0 / (1.0 + jnp.exp(-x))


def _gelu_tanh(x):
    c = math.sqrt(2.0 / math.pi)
    return 0.5 * x * (1.0 + jnp.tanh(c * (x + 0.044715 * (x * x * x))))


def _norm_proj_kernel(x_ref, g_ref, b_ref, *refs, plan):
    w_refs, o_ref = refs[:-1], refs[-1]
    for r0 in range(0, x_ref.shape[0], ROW_TILE):
        rows = slice(r0, r0 + ROW_TILE)
        h = _rms_norm(x_ref[rows, :], g_ref[...]).astype(BF16)

        def proj(wi, c0, h=h):
            return jnp.dot(h, w_refs[wi][:, c0:c0 + MIX_COL_CHUNK], preferred_element_type=F32)

        for kind, wi, wc, oc in plan:
            y = proj(wi, wc)
            if kind == "sigmoid":
                y = _sigmoid(y + b_ref[:, oc:oc + MIX_COL_CHUNK])
            elif kind == "qscale":
                y = y * DA_Q_SCALE
            elif kind == "gelu":
                y = _gelu_tanh(y)
            elif kind == "glu":
                y = y * _sigmoid(proj(wi, wc + MIX_COL_CHUNK))
            o_ref[rows, oc:oc + MIX_COL_CHUNK] = y.astype(o_ref.dtype)


def _norm_proj(x2d, g, weights, b, name, plan, tile=ROW_TILE):
    n, d = x2d.shape
    n_out = max(oc for _, _, _, oc in plan) + MIX_COL_CHUNK
    assert n % tile == 0 and tile % ROW_TILE == 0
    return pl.pallas_call(
        functools.partial(_norm_proj_kernel, plan=plan),
        grid=(n // tile,),
        in_specs=[
            pl.BlockSpec((tile, d), lambda i: (i, 0)),
            _resident((1, d)),
            _resident((1, b.shape[1])),
        ] + [_resident(w.shape) for w in weights],
        out_specs=pl.BlockSpec((tile, n_out), lambda i: (i, 0)),
        out_shape=jax.ShapeDtypeStruct((n, n_out), BF16),
        compiler_params=_params(("parallel",)),
        name=name,
    )(x2d, g, b, *weights)


def _mix_plan():
    w_in = N_BRANCH
    plan = [("glu", w_in, IN_CONV_A, COL_CONV), ("gelu", w_in, IN_LRU_G, COL_LRU_G)]
    plan += [("sigmoid", j, c, j * D_MODEL + c) for j in range(N_BRANCH) for c in range(0, D_MODEL, MIX_COL_CHUNK)]
    plan += [("qscale", w_in, IN_Q, COL_Q), ("plain", w_in, IN_K, COL_K), ("plain", w_in, IN_V, COL_V),
             ("plain", w_in, IN_LRU_X, COL_LRU_X)]
    return tuple(plan)


def _lru_rows(r0, carry, softplus, xa_ref, ri_ref, zg_ref, o_ref):
    rows = slice(r0, r0 + LRU_ROWS)
    xa = xa_ref[rows, :]
    r = _sigmoid(ri_ref[rows, 0:D_LRU])
    gi = _sigmoid(ri_ref[rows, D_LRU:2 * D_LRU])
    a = jnp.exp((-LRU_C) * r * softplus)
    u = jnp.sqrt(1.0 - a * a) * (gi * xa)
    groups = LRU_ROWS // SUBLANES
    a = a.reshape(groups, SUBLANES, D_LRU)
    u = u.reshape(groups, SUBLANES, D_LRU)
    row = lax.broadcasted_iota(jnp.int32, (groups, SUBLANES, D_LRU), 1)
    s = 1
    while s < SUBLANES:
        keep = row >= s
        u = jnp.where(keep, a * pltpu.roll(u, s, 1) + u, u)
        a = jnp.where(keep, a * pltpu.roll(a, s, 1), a)
        s *= 2
    hs = []
    for g in range(groups):
        h = a[g] * carry + u[g]
        hs.append(h)
        carry = jnp.broadcast_to(h[SUBLANES - 1:SUBLANES, :], (SUBLANES, D_LRU))
    gate = zg_ref[rows, :].astype(F32)
    o_ref[rows, :] = (jnp.concatenate(hs, axis=0) * gate).astype(o_ref.dtype)
    return carry


def _lru_kernel(zx_ref, zg_ref, cw_ref, cb_ref, wri_ref, bri_ref, lam_ref, o_ref,
                xe_ref, xa_ref, ri_ref, carry_ref, sp_ref):
    ts = zx_ref.shape[0]
    halo = SUBLANES

    @pl.when(pl.program_id(1) == 0)
    def _():
        xe_ref[0:halo, :] = jnp.zeros((halo, D_LRU), F32)
        carry_ref[...] = jnp.zeros((SUBLANES, D_LRU), F32)
        nlam = -lam_ref[...]
        sp_ref[...] = jnp.maximum(nlam, 0.0) + jnp.log(1.0 + jnp.exp(-jnp.abs(nlam)))

    xe_ref[halo:halo + ts, :] = zx_ref[...].astype(F32)
    xa = cb_ref[...] + cw_ref[LRU_CONV - 1:LRU_CONV, :] * xe_ref[halo:halo + ts, :]
    for j in range(LRU_CONV - 1):
        off = halo - (LRU_CONV - 1) + j
        xa = xa + cw_ref[j:j + 1, :] * xe_ref[off:off + ts, :]
    xe_ref[0:halo, :] = xe_ref[ts:ts + halo, :]
    xa_ref[...] = xa
    ri_ref[...] = jnp.dot(xa.astype(BF16), wri_ref[...], preferred_element_type=F32) + bri_ref[...]

    carry = carry_ref[...]
    for r0 in range(0, ts, LRU_ROWS):
        carry = _lru_rows(r0, carry, sp_ref[...], xa_ref, ri_ref, zg_ref, o_ref)
    carry_ref[...] = carry


def _lru_branch(zmix, batch, seq, cw, cb, wri, bri, lam):
    nt = seq // SEQ_TILE
    return pl.pallas_call(
        _lru_kernel,
        grid=(batch, nt),
        in_specs=[
            pl.BlockSpec((SEQ_TILE, D_LRU), lambda b, t: (b * nt + t, COL_LRU_X // D_LRU)),
            pl.BlockSpec((SEQ_TILE, D_LRU), lambda b, t: (b * nt + t, COL_LRU_G // D_LRU)),
            _resident((LRU_CONV, D_LRU)),
            _resident((1, D_LRU)),
            _resident((D_LRU, 2 * D_LRU)),
            _resident((1, 2 * D_LRU)),
            _resident((1, D_LRU)),
        ],
        out_specs=pl.BlockSpec((SEQ_TILE, D_LRU), lambda b, t: (b * nt + t, 0)),
        out_shape=jax.ShapeDtypeStruct((batch * seq, D_LRU), BF16),
        scratch_shapes=[
            pltpu.VMEM((SEQ_TILE + SUBLANES, D_LRU), F32),
            pltpu.VMEM((SEQ_TILE, D_LRU), F32),
            pltpu.VMEM((SEQ_TILE, 2 * D_LRU), F32),
            pltpu.VMEM((SUBLANES, D_LRU), F32),
            pltpu.VMEM((1, D_LRU), F32),
        ],
        compiler_params=_params(("parallel", "arbitrary")),
        name="lru_branch",
    )(zmix, zmix, cw, cb, wri, bri, lam)


def _cconv_rows(r0, b_ref, o_ref, ce_ref, sh_ref, wb_ref):
    groups = range(0, CONV_ROWS, SUBLANES)
    accs = [jnp.broadcast_to(b_ref[...], (SUBLANES, D_CONV)) for _ in groups]
    for j in range(CONV_WIDTH):
        r = (CONV_FIRST_TAP + j) % SUBLANES
        base = r0 + CONV_FIRST_TAP + j - r
        wb = wb_ref[j]
        for k, g0 in enumerate(groups):
            lo = base + g0
            src = ce_ref[lo:lo + SUBLANES, :] if r == 0 else sh_ref[r - 1, lo:lo + SUBLANES, :]
            accs[k] = accs[k] + wb * src
    o_ref[r0:r0 + CONV_ROWS, :] = jnp.concatenate(accs, axis=0).astype(o_ref.dtype)


def _cconv_kernel(c_ref, w_ref, b_ref, o_ref, ce_ref, sh_ref, wb_ref):
    ts = c_ref.shape[0]

    @pl.when(pl.program_id(1) == 0)
    def _():
        ce_ref[0:CONV_HALO, :] = jnp.zeros((CONV_HALO, D_CONV), F32)
        for j in range(CONV_WIDTH):
            wb_ref[j] = jnp.broadcast_to(w_ref[j:j + 1, :], (SUBLANES, D_CONV))

    ce_ref[CONV_HALO:CONV_HALO + ts, :] = c_ref[...].astype(F32)
    for r in range(1, SUBLANES):
        sh_ref[r - 1] = ce_ref[r:r + sh_ref.shape[1], :]
    for r0 in range(0, ts, CONV_ROWS):
        _cconv_rows(r0, b_ref, o_ref, ce_ref, sh_ref, wb_ref)
    ce_ref[0:CONV_HALO, :] = ce_ref[ts:ts + CONV_HALO, :]


def _cconv_branch(zmix, batch, seq, w, b):
    nt = seq // SEQ_TILE
    return pl.pallas_call(
        _cconv_kernel,
        grid=(batch, nt),
        in_specs=[
            pl.BlockSpec((SEQ_TILE, D_CONV), lambda b_, t: (b_ * nt + t, COL_CONV // D_CONV)),
            _resident((CONV_WIDTH, D_CONV)),
            _resident((1, D_CONV)),
        ],
        out_specs=pl.BlockSpec((SEQ_TILE, D_CONV), lambda b_, t: (b_ * nt + t, 0)),
        out_shape=jax.ShapeDtypeStruct((batch * seq, D_CONV), BF16),
        scratch_shapes=[
            pltpu.VMEM((SEQ_TILE + CONV_HALO, D_CONV), F32),
            pltpu.VMEM((SUBLANES - 1, SEQ_TILE + CONV_HALO - SUBLANES, D_CONV), F32),
            pltpu.VMEM((CONV_WIDTH, SUBLANES, D_CONV), F32),
        ],
        compiler_params=_params(("parallel", "arbitrary")),
        name="cconv_branch",
    )(zmix, w, b)


def _t5_bucket(rel):
    nb = REL_BUCKETS // 2
    ret = jnp.where(rel > 0, nb, 0)
    n = jnp.abs(rel)
    max_exact = nb // 2
    large = max_exact + (jnp.log(jnp.maximum(n, 1).astype(F32) / max_exact)
                         / math.log(REL_MAX_DIST / max_exact) * (nb - max_exact)).astype(jnp.int32)
    large = jnp.minimum(large, nb - 1)
    return ret + jnp.where(n < max_exact, n, large)


def _bias_tiles_kernel(tbl_ref, bucket_ref, o_ref, *, far_bucket):
    h = pl.program_id(0)
    bucket = bucket_ref[...]
    acc = jnp.zeros(bucket.shape, F32)
    for bkt in range(REL_BUCKETS):
        acc = jnp.where(bucket == bkt, tbl_ref[bkt, h], acc)
    acc = (acc - tbl_ref[far_bucket, h]) * LOG2E
    o_ref[...] = jnp.where(bucket < 0, MASK_VALUE, acc)


def _bias_tiles(rel_bias):
    kpos = jnp.arange(DA_TK, dtype=jnp.int32)[:, None]
    qpos = jnp.arange(DA_TQ, dtype=jnp.int32)[None, :]
    prev = _t5_bucket(kpos - DA_TK - qpos)
    diag = jnp.where(kpos // CHUNK <= qpos // CHUNK, _t5_bucket(kpos - qpos), -1)
    buckets = jnp.stack([prev, diag])
    assert DA_TK >= REL_MAX_DIST and DA_TQ == DA_TK
    far_bucket = REL_BUCKETS // 2 - 1
    return pl.pallas_call(
        functools.partial(_bias_tiles_kernel, far_bucket=far_bucket),
        grid=(DA_HEADS,),
        in_specs=[
            pl.BlockSpec(memory_space=pltpu.SMEM),
            pl.BlockSpec((2, DA_TK, DA_TQ), lambda h: (0, 0, 0)),
        ],
        out_specs=pl.BlockSpec((None, 2, DA_TK, DA_TQ), lambda h: (h, 0, 0, 0)),
        out_shape=jax.ShapeDtypeStruct((DA_HEADS, 2, DA_TK, DA_TQ), F32),
        compiler_params=_params(("parallel",)),
        name="bias_tiles",
    )(rel_bias, buckets)


def _da_kernel(q_ref, k_ref, v_ref, bias_ref, lam_ref, sg_ref, o_ref,
               vt_ref, m_ref, l_ref, acc_ref, s_ref, *, lam_init):
    tq, tk = DA_TQ, DA_TK
    blk = 2 * DA_DIM
    chains = [(h, c) for h in range(DA_HEADS) for c in range(2)]

    ones = jnp.ones((DA_ONES_ROWS, 2 * tk), BF16)
    for h in range(DA_HEADS):
        for jp in range(vt_ref.shape[1]):
            for half in range(2):
                r0 = (2 * jp + half) * tk
                v = v_ref[r0:r0 + tk, h * DA_VDIM:(h + 1) * DA_VDIM]
                vt_ref[h, jp, 0:DA_VDIM, half * tk:(half + 1) * tk] = v.astype(F32).T.astype(BF16)
            vt_ref[h, jp, DA_VDIM:DA_VDIM + DA_ONES_ROWS, :] = ones

    lv = lam_ref[...]
    lam = (jnp.exp(jnp.sum(lv[0:1, :] * lv[1:2, :], axis=-1, keepdims=True))
           - jnp.exp(jnp.sum(lv[2:3, :] * lv[3:4, :], axis=-1, keepdims=True)) + lam_init)
    lane = lax.broadcasted_iota(jnp.int32, (tq, blk), 1)

    def query_tile(i, carry):
        _da_query_tile(i, lam, lane, chains, q_ref, k_ref, bias_ref, sg_ref, o_ref,
                       vt_ref, m_ref, l_ref, acc_ref, s_ref, lam_init)
        return carry

    lax.fori_loop(0, q_ref.shape[0] // tq, query_tile, 0)


def _da_query_tile(i, lam, lane, chains, q_ref, k_ref, bias_ref, sg_ref, o_ref,
                   vt_ref, m_ref, l_ref, acc_ref, s_ref, lam_init):
    tq, tk = DA_TQ, DA_TK
    blk = 2 * DA_DIM
    q0 = pl.multiple_of(i * tq, tq)
    qz = []
    for h, c in chains:
        q = q_ref[pl.ds(q0, tq), h * blk:(h + 1) * blk]
        qz.append(jnp.where((lane >= DA_DIM) == (c == 1), q, jnp.zeros_like(q)))

    m_ref[...] = jnp.full(m_ref.shape, MASK_VALUE, F32)
    l_ref[...] = jnp.zeros(l_ref.shape, F32)
    acc_ref[...] = jnp.zeros(acc_ref.shape, F32)

    def step(jp, nears):
        nkeys = len(nears) * tk
        k0 = pl.multiple_of(jp * (2 * tk), 2 * tk)

        def logits(n):
            h = chains[n][0]
            kb = k_ref[pl.ds(k0, nkeys), h * blk:(h + 1) * blk]
            s = lax.dot_general(kb, qz[n], (((1,), (1,)), ((), ())), preferred_element_type=F32)
            if all(near is None for near in nears):
                return s
            parts = [s[t * tk:(t + 1) * tk, :] if near is None else s[t * tk:(t + 1) * tk, :] + bias_ref[h, near]
                     for t, near in enumerate(nears)]
            return jnp.concatenate(parts, axis=0)

        block_max = []
        for n in range(len(chains)):
            s = logits(n)
            s_ref[n, 0:nkeys, :] = s
            block_max.append(jnp.max(s, axis=0, keepdims=True))
        alphas, probs = [], []
        for n in range(len(chains)):
            m_prev = m_ref[n]
            m_new = jnp.maximum(m_prev, block_max[n])
            alphas.append(jnp.exp2(m_prev - m_new))
            probs.append(jnp.exp2(s_ref[n, 0:nkeys, :] - m_new).astype(BF16))
            m_ref[n] = m_new
        for n, (h, c) in enumerate(chains):
            pv = jnp.dot(vt_ref[h, jp, :, 0:nkeys], probs[n], preferred_element_type=F32)
            acc_ref[n] = alphas[n] * acc_ref[n] + pv[0:DA_VDIM, :]
            l_ref[n] = alphas[n] * l_ref[n] + pv[DA_VDIM:DA_VDIM + 1, :]

    def far_body(jp, carry):
        step(jp, (None, None))
        return carry

    lax.fori_loop(0, jnp.maximum(i - 1, 0) // 2, far_body, 0)
    odd = (i % 2) == 1

    @pl.when(odd)
    def _():
        step((i - 1) // 2, (0, 1))

    @pl.when(jnp.logical_and(jnp.logical_not(odd), i >= 2))
    def _():
        step((i - 2) // 2, (None, 0))

    @pl.when(jnp.logical_not(odd))
    def _():
        step(i // 2, (1,))

    for h in range(DA_HEADS):
        d = (acc_ref[2 * h] * (1.0 / l_ref[2 * h])
             - lam * (acc_ref[2 * h + 1] * (1.0 / l_ref[2 * h + 1])))
        y = d * lax.rsqrt(jnp.mean(d * d, axis=0, keepdims=True) + EPS)
        o_ref[pl.ds(q0, tq), h * DA_VDIM:(h + 1) * DA_VDIM] = (
            y.T * sg_ref[...] * (1.0 - lam_init)).astype(o_ref.dtype)


def _diff_attention(zmix, batch, seq, bias_tiles, lam_vec, subln_g, lam_init):
    assert COL_Q % D_DA == 0 and COL_K % D_DA == 0 and COL_V % D_DA == 0
    assert seq % (2 * DA_TK) == 0 and seq % DA_TQ == 0
    return pl.pallas_call(
        functools.partial(_da_kernel, lam_init=lam_init),
        grid=(batch,),
        in_specs=[
            pl.BlockSpec((seq, D_DA), lambda b: (b, COL_Q // D_DA)),
            pl.BlockSpec((seq, D_DA), lambda b: (b, COL_K // D_DA)),
            pl.BlockSpec((seq, D_DA), lambda b: (b, COL_V // D_DA)),
            _resident((DA_HEADS, 2, DA_TK, DA_TQ)),
            _resident((4, DA_DIM)),
            _resident((1, DA_VDIM)),
        ],
        out_specs=pl.BlockSpec((seq, D_DA), lambda b: (b, 0)),
        out_shape=jax.ShapeDtypeStruct((batch * seq, D_DA), BF16),
        scratch_shapes=[
            pltpu.VMEM((DA_HEADS, seq // (2 * DA_TK), DA_VDIM + DA_ONES_ROWS, 2 * DA_TK), BF16),
            pltpu.VMEM((2 * DA_HEADS, 1, DA_TQ), F32),
            pltpu.VMEM((2 * DA_HEADS, 1, DA_TQ), F32),
            pltpu.VMEM((2 * DA_HEADS, DA_VDIM, DA_TQ), F32),
            pltpu.VMEM((2 * DA_HEADS, 2 * DA_TK, DA_TQ), F32),
        ],
        compiler_params=_params(("parallel",)),
        name="diff_attention",
    )(zmix, zmix, zmix, bias_tiles, lam_vec, subln_g)


def _merge_xattn_kernel(x_ref, ya_ref, yb_ref, yc_ref, g0_ref, g1_ref, g2_ref, lg_ref, lb_ref,
                        wa_ref, wb_ref, wc_ref, wo_ref, gx_ref, wq_ref, kv_ref, wxo_ref, o_ref, oc_ref):
    for r0 in range(0, x_ref.shape[0], ROW_TILE):
        rows = slice(r0, r0 + ROW_TILE)
        c = yb_ref[rows, :].astype(F32)
        cc = c - jnp.mean(c, axis=-1, keepdims=True)
        cn = cc * lax.rsqrt(jnp.mean(cc * cc, axis=-1, keepdims=True) + EPS) * lg_ref[...] + lb_ref[...]
        yb = (cn * _sigmoid(cn)).astype(BF16)
        merged = (g0_ref[rows, :].astype(F32) * jnp.dot(ya_ref[rows, :], wa_ref[...], preferred_element_type=F32)
                  + g1_ref[rows, :].astype(F32) * jnp.dot(yb, wb_ref[...], preferred_element_type=F32)
                  + g2_ref[rows, :].astype(F32) * jnp.dot(yc_ref[rows, :], wc_ref[...], preferred_element_type=F32))
        x = x_ref[rows, :] + jnp.dot(merged.astype(BF16), wo_ref[...], preferred_element_type=F32)

        hq = _rms_norm(x, gx_ref[...]).astype(BF16)
        q = (jnp.dot(hq, wq_ref[...], preferred_element_type=F32) * (XA_DIM ** -0.5)).astype(BF16)
        for h in range(XA_HEADS):
            c0 = h * XA_DIM
            kh = kv_ref[:, c0:c0 + XA_DIM]
            vh = kv_ref[:, D_MODEL + c0:D_MODEL + c0 + XA_DIM]
            s = lax.dot_general(q[:, c0:c0 + XA_DIM], kh, (((1,), (1,)), ((), ())), preferred_element_type=F32)
            p = jnp.exp(s - jnp.max(s, axis=-1, keepdims=True))
            l = jnp.sum(p, axis=-1, keepdims=True)
            o = jnp.dot(p.astype(BF16), vh, preferred_element_type=F32) / l
            oc_ref[rows, c0:c0 + XA_DIM] = o.astype(BF16)
        o_ref[rows, :] = x + jnp.dot(oc_ref[rows, :], wxo_ref[...], preferred_element_type=F32)


def _merge_xattn(x2d, batch, seq, ya, yb, yc, zmix, lg, lb, wa, wb, wc, wo, gx, wq, kv, wxo):
    nt = seq // MERGE_TILE
    n_mem = kv.shape[0] // batch
    tile = lambda width, col=0: pl.BlockSpec((MERGE_TILE, width), lambda b, t: (b * nt + t, col))
    return pl.pallas_call(
        _merge_xattn_kernel,
        grid=(batch, nt),
        in_specs=[
            tile(D_MODEL),
            tile(D_LRU),
            tile(D_CONV),
            tile(D_DA),
            tile(D_MODEL, 0),
            tile(D_MODEL, 1),
            tile(D_MODEL, 2),
            _resident((1, D_CONV)),
            _resident((1, D_CONV)),
            _resident((D_LRU, D_MODEL)),
            _resident((D_CONV, D_MODEL)),
            _resident((D_DA, D_MODEL)),
            _resident((D_MODEL, D_MODEL)),
            _resident((1, D_MODEL)),
            _resident((D_MODEL, D_MODEL)),
            pl.BlockSpec((n_mem, 2 * D_MODEL), lambda b, t: (b, 0)),
            _resident((D_MODEL, D_MODEL)),
        ],
        out_specs=tile(D_MODEL),
        out_shape=jax.ShapeDtypeStruct(x2d.shape, F32),
        scratch_shapes=[pltpu.VMEM((MERGE_TILE, D_MODEL), BF16)],
        compiler_params=_params(("parallel", "arbitrary")),
        name="merge_xattn",
    )(x2d, ya, yb, yc, zmix, zmix, zmix, lg, lb, wa, wb, wc, wo, gx, wq, kv, wxo)


def _ffn_kernel(x_ref, g_ref, w1_ref, w3_ref, cw_ref, cb_ref, w2_ref, fg_ref, o_ref,
                ae_ref, halo_ref, hm_ref, *, final_norm):
    halo = SUBLANES
    ts = SEQ_TILE

    @pl.when(pl.program_id(1) == 0)
    def _():
        halo_ref[...] = jnp.zeros(halo_ref.shape, F32)

    for r0 in range(0, x_ref.shape[0], ts):
        rows = slice(r0, r0 + ts)
        x = x_ref[rows, :]
        hf = _rms_norm(x, g_ref[...]).astype(BF16)
        for c0 in range(0, D_FF, FFN_COL_CHUNK):
            cs = slice(c0, c0 + FFN_COL_CHUNK)
            a = jnp.dot(hf, w1_ref[:, cs], preferred_element_type=F32)
            ae_ref[0:halo, :] = halo_ref[:, cs]
            ae_ref[halo:halo + ts, :] = a
            halo_ref[:, cs] = a[ts - halo:ts, :]
            y = cb_ref[:, cs] + cw_ref[FFN_CONV - 1:FFN_CONV, cs] * a
            for j in range(FFN_CONV - 1):
                off = halo - (FFN_CONV - 1) + j
                y = y + cw_ref[j:j + 1, cs] * ae_ref[off:off + ts, :]
            up = jnp.dot(hf, w3_ref[:, cs], preferred_element_type=F32)
            hm_ref[rows, cs] = (y * _sigmoid(y) * up).astype(BF16)
        out = x + jnp.dot(hm_ref[rows, :], w2_ref[...], preferred_element_type=F32)
        if final_norm:
            out = _rms_norm(out, fg_ref[...])
        o_ref[rows, :] = out


def _ffn(x2d, batch, seq, g, w1, w3, cw, cb, w2, fg, final_norm):
    nt = seq // FFN_TILE
    return pl.pallas_call(
        functools.partial(_ffn_kernel, final_norm=final_norm),
        grid=(batch, nt),
        in_specs=[
            pl.BlockSpec((FFN_TILE, D_MODEL), lambda b, t: (b * nt + t, 0)),
            _resident((1, D_MODEL)),
            _resident((D_MODEL, D_FF)),
            _resident((D_MODEL, D_FF)),
            _resident((FFN_CONV, D_FF)),
            _resident((1, D_FF)),
            _resident((D_FF, D_MODEL)),
            _resident((1, D_MODEL)),
        ],
        out_specs=pl.BlockSpec((FFN_TILE, D_MODEL), lambda b, t: (b * nt + t, 0)),
        out_shape=jax.ShapeDtypeStruct(x2d.shape, F32),
        scratch_shapes=[
            pltpu.VMEM((SEQ_TILE + SUBLANES, FFN_COL_CHUNK), F32),
            pltpu.VMEM((SUBLANES, D_FF), F32),
            pltpu.VMEM((FFN_TILE, D_FF), BF16),
        ],
        compiler_params=_params(("parallel", "arbitrary")),
        name="ffn_final" if final_norm else "ffn",
    )(x2d, g, w1, w3, cw, cb, w2, fg)


def _block_diag(w):
    nb, bi, bj = w.shape
    eye = jnp.eye(nb, dtype=w.dtype)
    return jnp.einsum('hij,hg->higj', w, eye).reshape(nb * bi, nb * bj)


def kernel(x, mem, rel_bias, norm_mix_g, w_in, w_gate, b_gate, lru_conv_w, lru_conv_b, lru_wr, lru_br, lru_wi, lru_bi, lru_lambda, lru_out, cm_conv_w, cm_conv_b, cm_ln_g, cm_ln_b, cm_out, da_lambda, da_subln_g, da_out, w_o, norm_xa_g, norm_mem_g, xa_wq, xa_wkv, xa_wo, norm_ffn_g, ffn_w1, ffn_w3, ffn_conv_w, ffn_conv_b, ffn_w2, final_g):
    batch, seq, d = x.shape
    n_mem = mem.shape[1]
    depth = w_in.shape[0]
    assert d == D_MODEL and seq % SEQ_TILE == 0 and seq % ROW_TILE == 0 and seq % DA_TQ == 0
    assert w_in.shape[2] == D_IN

    bias_tiles = _bias_tiles(rel_bias)
    x2d = x.reshape(batch * seq, d)
    mem2d = mem.reshape(batch * n_mem, d)
    row = lambda v: v.reshape(1, -1)
    no_bias = jnp.zeros((1, MIX_COL_CHUNK), F32)
    kv_plan = tuple(("plain", 0, c, c) for c in range(0, 2 * D_MODEL, MIX_COL_CHUNK))

    for l in range(depth):
        w_mix = [w_gate[l, j].astype(BF16) for j in range(N_BRANCH)] + [w_in[l].astype(BF16)]
        zmix = _norm_proj(x2d, row(norm_mix_g[l]), w_mix, row(b_gate[l]), "mix_proj", _mix_plan(),
                          tile=MIX_TILE)
        wri = jnp.concatenate([_block_diag(lru_wr[l]), _block_diag(lru_wi[l])], axis=1).astype(BF16)
        bri = jnp.concatenate([lru_br[l], lru_bi[l]]).reshape(1, -1)
        ya = _lru_branch(zmix, batch, seq, lru_conv_w[l], row(lru_conv_b[l]), wri, bri, row(lru_lambda[l]))
        yb = _cconv_branch(zmix, batch, seq, cm_conv_w[l], row(cm_conv_b[l]))
        lam_init = 0.8 - 0.6 * math.exp(-0.3 * l)
        yc = _diff_attention(zmix, batch, seq, bias_tiles, da_lambda[l], row(da_subln_g[l]), lam_init)
        kv = _norm_proj(mem2d, row(norm_mem_g[l]), [xa_wkv[l].astype(BF16)], no_bias, "mem_kv", kv_plan)
        x2d = _merge_xattn(x2d, batch, seq, ya, yb, yc, zmix, row(cm_ln_g[l]), row(cm_ln_b[l]),
                           lru_out[l].astype(BF16), cm_out[l].astype(BF16), da_out[l].astype(BF16),
                           w_o[l].astype(BF16), row(norm_xa_g[l]), xa_wq[l].astype(BF16), kv,
                           xa_wo[l].astype(BF16))
        x2d = _ffn(x2d, batch, seq, row(norm_ffn_g[l]), ffn_w1[l].astype(BF16), ffn_w3[l].astype(BF16),
                   ffn_conv_w[l], row(ffn_conv_b[l]), ffn_w2[l].astype(BF16), row(final_g),
                   final_norm=(l == depth - 1))
    return x2d.reshape(batch, seq, d)
```
